```python
import math
import jax
import jax.numpy as jnp
from jax import lax
import numpy as np

D_MODEL = 2048
BATCH = 4
SEQ = 2048
DEPTH = 2
DEC_BATCH = 128
DEC_SEQ = 8
PAST_LEN = 8192
PAGE_SIZE = 128

HEAD_DIM = 128
N_HEADS_TOTAL = D_MODEL // HEAD_DIM
HA = N_HEADS_TOTAL // 4
HC = N_HEADS_TOTAL // 4
HB = N_HEADS_TOTAL - HA - HC
DA = HEAD_DIM
H_IDX = 8
D_IDX = 64
TOPK_MAX = 256
Q_LORA = D_MODEL // 4
KV_LORA = D_MODEL // 8
DN = 128
DR = 64
DV = HEAD_DIM
HC_KV = 2
C_GROUP = HC // HC_KV
DC_QK = HEAD_DIM // 2
DC_V = HEAD_DIM
D_FF = ((8 * D_MODEL + 3 * 256 - 1) // (3 * 256)) * 256
N_BUCKETS = 32
MAX_DISTANCE = 128
ROPE_THETA = 10000.0
EPS = 1e-6
QBLOCK = 128

IN_SPLITS = (HA * HEAD_DIM, DA, DA, H_IDX * D_IDX, D_IDX, H_IDX, Q_LORA, KV_LORA + DR,
             HC * 2 * DC_QK, HC_KV * 2 * DC_QK, HC_KV * DC_V)
IN_COLS = sum(IN_SPLITS)
IN_OFFSETS = tuple(int(v) for v in np.cumsum(IN_SPLITS)[:-1])

kernel_name = "hybrid_dsa_mla_diff_step"


def rmsnorm(x, g):
    xf = x.astype(jnp.float32)
    y = xf * lax.rsqrt(jnp.mean(xf * xf, axis=-1, keepdims=True) + EPS)
    return (y * g.astype(jnp.float32)).astype(x.dtype)


def t5_bias(table, rel):
    n = jnp.maximum(-rel, 0)
    max_exact = N_BUCKETS // 2
    nf = jnp.maximum(n, 1).astype(jnp.float32)
    large = max_exact + (jnp.log(nf / max_exact) / math.log(MAX_DISTANCE / max_exact)
                         * (N_BUCKETS - max_exact)).astype(jnp.int32)
    bucket = jnp.where(n < max_exact, n, jnp.minimum(large, N_BUCKETS - 1))
    return table[bucket].astype(jnp.float32)


def rope(x, pos):
    half = DR // 2
    inv = jnp.power(ROPE_THETA, -jnp.arange(half, dtype=jnp.float32) / half)
    ang = pos.astype(jnp.float32)[:, None] * inv[None, :]
    ang = ang.reshape((ang.shape[0],) + (1,) * (x.ndim - 3) + (half,))
    cos, sin = jnp.cos(ang), jnp.sin(ang)
    x1 = x[..., :half].astype(jnp.float32)
    x2 = x[..., half:].astype(jnp.float32)
    return jnp.concatenate([x1 * cos - x2 * sin, x1 * sin + x2 * cos], axis=-1).astype(x.dtype)


def to_blocks(x):
    b, t = x.shape[:2]
    return jnp.moveaxis(x.reshape((b, t // QBLOCK, QBLOCK) + x.shape[2:]), 1, 0)


def from_blocks(x):
    nb, b, q = x.shape[:3]
    return jnp.moveaxis(x, 0, 1).reshape((b, nb * q) + x.shape[3:])


def project(x, pos, g_pre, w_in_l, g_qa, w_q_b_l, g_kva, w_uk_l):
    b, t = x.shape[:2]
    h = rmsnorm(x, g_pre) @ w_in_l
    aq, ak, av, iq, ik, iw, bq, bkv, cq, ck, cv = jnp.split(h, IN_OFFSETS, axis=-1)
    qb = (rmsnorm(bq, g_qa) @ w_q_b_l).reshape(b, t, HB, DN + DR)
    q_lat = jnp.einsum("bthn,chn->bthc", qb[..., :DN], w_uk_l)
    q_pe = rope(qb[..., DN:], pos)
    b_lat = rmsnorm(bkv[..., :KV_LORA], g_kva)
    b_kpe = rope(bkv[..., KV_LORA:], pos)
    queries = (aq.reshape(b, t, HA, HEAD_DIM), iq.reshape(b, t, H_IDX, D_IDX), iw,
               q_lat, q_pe, cq.reshape(b, t, HC, 2 * DC_QK))
    rows = (ak, av, ik, b_lat, b_kpe,
            ck.reshape(b, t, HC_KV, 2 * DC_QK), cv.reshape(b, t, HC_KV, DC_V))
    return queries, rows


def dsa_attend(aq, iq, iw, qpos, kidx_all, kpos_all, gather_kv, topk, table):
    f32 = jnp.float32
    rel = jnp.einsum("bqhd,bld->bqhl", iq.astype(f32), kidx_all.astype(f32)) * (D_IDX ** -0.5)
    score = jnp.einsum("bqhl,bqh->bql", jax.nn.relu(rel), iw.astype(f32)) * (H_IDX ** -0.5)
    admissible = kpos_all[None, :] <= qpos[:, None]
    score = jnp.where(admissible[None], score, -jnp.inf)
    top_val, top_idx = lax.top_k(score, topk)
    valid = top_val > -jnp.inf
    k_sel, v_sel = gather_kv(top_idx)
    bias = t5_bias(table, kpos_all[top_idx] - qpos[None, :, None])
    logits = jnp.einsum("bqhd,bqkd->bhqk", aq.astype(f32), k_sel.astype(f32)) * (DA ** -0.5)
    logits = jnp.where(valid[:, None], logits + jnp.transpose(bias, (0, 3, 1, 2)), -jnp.inf)
    p = jax.nn.softmax(logits, axis=-1)
    return jnp.einsum("bhqk,bqkd->bqhd", p, v_sel.astype(f32)).astype(aq.dtype)


def mla_attend(q_lat, q_pe, qpos, lat_all, kpe_all, kpos_all):
    f32 = jnp.float32
    lat = lat_all.astype(f32)
    logits = (jnp.einsum("bqhc,blc->bhql", q_lat.astype(f32), lat)
              + jnp.einsum("bqhr,blr->bhql", q_pe.astype(f32), kpe_all.astype(f32))) * ((DN + DR) ** -0.5)
    logits = jnp.where(kpos_all[None, :] <= qpos[:, None], logits, -jnp.inf)
    p = jax.nn.softmax(logits, axis=-1)
    return jnp.einsum("bhql,blc->bqhc", p, lat).astype(q_lat.dtype)


def diff_attend(cq, qpos, ck_all, cv_all, kpos_all, lam, table):
    f32 = jnp.float32
    b, q = cq.shape[:2]
    l = ck_all.shape[1]
    qq = cq.reshape(b, q, HC_KV, C_GROUP, 2, DC_QK).astype(f32)
    kk = ck_all.reshape(b, l, HC_KV, 2, DC_QK).astype(f32)
    logits = jnp.einsum("bqkgmd,blkmd->bmkgql", qq, kk) * (DC_QK ** -0.5)
    bias = t5_bias(table, kpos_all[None, :] - qpos[:, None]).reshape(q, l, HC_KV, C_GROUP)
    logits = logits + jnp.transpose(bias, (2, 3, 0, 1))
    logits = jnp.where(kpos_all[None, :] <= qpos[:, None], logits, -jnp.inf)
    p = jax.nn.softmax(logits, axis=-1)
    w = p[:, 0] - lam * p[:, 1]
    out = jnp.einsum("bkgql,blkv->bqkgv", w, cv_all.astype(f32))
    return out.reshape(b, q, HC, DC_V).astype(cq.dtype)


def mix_and_ffn(x, a_o, b_lat_o, c_o, lam_init, w_uv_l, g_sub, w_out_l, g_post_mix,
                g_pre_ffn, w_gate_l, w_up_l, w_down_l, g_post_ffn):
    b, t = x.shape[:2]
    b_o = jnp.einsum("bthc,chv->bthv", b_lat_o, w_uv_l)
    c_o = rmsnorm(c_o, g_sub) * (1.0 - lam_init)
    mix = jnp.concatenate([a_o.reshape(b, t, HA * HEAD_DIM), b_o.reshape(b, t, HB * DV),
                           c_o.reshape(b, t, HC * DC_V)], axis=-1)
    x = x + rmsnorm(mix @ w_out_l, g_post_mix)
    hn = rmsnorm(x, g_pre_ffn)
    f = (jax.nn.silu(hn @ w_gate_l) * (hn @ w_up_l)) @ w_down_l
    return x + rmsnorm(f, g_post_ffn)


def setup_inputs(seed: int = 0) -> dict:
    key = jax.random.key(seed)
    ks = jax.random.split(key, 40)
    f32 = jnp.float32
    n_pages = PAST_LEN // PAGE_SIZE
    n_phys = (DEC_BATCH * n_pages * 5) // 4
    pool = (DEPTH, n_phys, PAGE_SIZE)

    def nrm(k, shape, scale):
        return jax.random.normal(k, shape, f32) * scale

    def gain(k, shape):
        return 1.0 + 0.02 * jax.random.normal(k, shape, f32)

    page_table = jax.random.permutation(ks[9], n_phys)[: DEC_BATCH * n_pages]
    page_table = page_table.reshape(DEC_BATCH, n_pages).astype(jnp.int32)
    return {
        "x_prompt": nrm(ks[0], (BATCH, SEQ, D_MODEL), 1.0),
        "x_sample": nrm(ks[1], (DEC_BATCH, DEC_SEQ, D_MODEL), 1.0),
        "cache_a_k": nrm(ks[2], pool + (DA,), 1.0),
        "cache_a_v": nrm(ks[3], pool + (DA,), 1.0),
        "cache_a_kidx": nrm(ks[4], pool + (D_IDX,), 1.0),
        "cache_b_latent": nrm(ks[5], pool + (KV_LORA,), 1.0),
        "cache_b_krope": nrm(ks[6], pool + (DR,), 1.0),
        "cache_c_k": nrm(ks[7], pool + (HC_KV, 2 * DC_QK), 1.0),
        "cache_c_v": nrm(ks[8], pool + (HC_KV, DC_V), 1.0),
        "page_table": page_table,
        "rel_bias": nrm(ks[10], (N_BUCKETS, HA + HC), 0.2),
        "norm_pre_mix": gain(ks[11], (DEPTH, D_MODEL)),
        "w_in": nrm(ks[12], (DEPTH, D_MODEL, IN_COLS), D_MODEL ** -0.5),
        "q_a_norm": gain(ks[13], (DEPTH, Q_LORA)),
        "w_q_b": nrm(ks[14], (DEPTH, Q_LORA, HB * (DN + DR)), Q_LORA ** -0.5),
        "kv_a_norm": gain(ks[15], (DEPTH, KV_LORA)),
        "w_uk": nrm(ks[16], (DEPTH, KV_LORA, HB, DN), KV_LORA ** -0.5),
        "w_uv": nrm(ks[17], (DEPTH, KV_LORA, HB, DV), KV_LORA ** -0.5),
        "lambda_q1": nrm(ks[18], (DEPTH, DC_QK), 0.1),
        "lambda_k1": nrm(ks[19], (DEPTH, DC_QK), 0.1),
        "lambda_q2": nrm(ks[20], (DEPTH, DC_QK), 0.1),
        "lambda_k2": nrm(ks[21], (DEPTH, DC_QK), 0.1),
        "c_subln": gain(ks[22], (DEPTH, DC_V)),
        "w_out": nrm(ks[23], (DEPTH, D_MODEL, D_MODEL), D_MODEL ** -0.5),
        "norm_post_mix": gain(ks[24], (DEPTH, D_MODEL)),
        "norm_pre_ffn": gain(ks[25], (DEPTH, D_MODEL)),
        "w_gate": nrm(ks[26], (DEPTH, D_MODEL, D_FF), D_MODEL ** -0.5),
        "w_up": nrm(ks[27], (DEPTH, D_MODEL, D_FF), D_MODEL ** -0.5),
        "w_down": nrm(ks[28], (DEPTH, D_FF, D_MODEL), D_FF ** -0.5),
        "norm_post_ffn": gain(ks[29], (DEPTH, D_MODEL)),
    }


def reference(x_prompt, x_sample, cache_a_k, cache_a_v, cache_a_kidx, cache_b_latent, cache_b_krope,
              cache_c_k, cache_c_v, page_table, rel_bias, norm_pre_mix, w_in, q_a_norm, w_q_b,
              kv_a_norm, w_uk, w_uv, lambda_q1, lambda_k1, lambda_q2, lambda_k2, c_subln, w_out,
              norm_post_mix, norm_pre_ffn, w_gate, w_up, w_down, norm_post_ffn):
    f32 = jnp.float32
    pos_p = jnp.arange(SEQ, dtype=jnp.int32)
    pos_s = PAST_LEN + jnp.arange(DEC_SEQ, dtype=jnp.int32)
    kpos_s = jnp.arange(PAST_LEN + DEC_SEQ, dtype=jnp.int32)
    topk_p = min(TOPK_MAX, SEQ // 4)
    topk_s = min(TOPK_MAX, (PAST_LEN + DEC_SEQ) // 4)
    bias_a = rel_bias[:, :HA]
    bias_c = rel_bias[:, HA:]
    bidx_p = jnp.arange(x_prompt.shape[0])[:, None, None]
    bidx_s = jnp.arange(x_sample.shape[0])[:, None, None]
    new_p = [[] for _ in range(7)]
    new_s = [[] for _ in range(7)]
    hp, hs = x_prompt, x_sample
    for l in range(DEPTH):
        lam_init = 0.8 - 0.6 * math.exp(-0.3 * l)
        lam = (jnp.exp(jnp.sum(lambda_q1[l].astype(f32) * lambda_k1[l].astype(f32)))
               - jnp.exp(jnp.sum(lambda_q2[l].astype(f32) * lambda_k2[l].astype(f32))) + lam_init)

        (aq, iq, iw, q_lat, q_pe, cq), rows_p = project(
            hp, pos_p, norm_pre_mix[l], w_in[l], q_a_norm[l], w_q_b[l], kv_a_norm[l], w_uk[l])
        ak, av, ik, b_lat, b_kpe, ck, cv = rows_p

        def gather_p(idx):
            return ak[bidx_p, idx], av[bidx_p, idx]

        def block_p(blk):
            aq_b, iq_b, iw_b, qlat_b, qpe_b, cq_b, qpos_b = blk
            a_b = dsa_attend(aq_b, iq_b, iw_b, qpos_b, ik, pos_p, gather_p, topk_p, bias_a)
            b_b = mla_attend(qlat_b, qpe_b, qpos_b, b_lat, b_kpe, pos_p)
            c_b = diff_attend(cq_b, qpos_b, ck, cv, pos_p, lam, bias_c)
            return a_b, b_b, c_b

        a_o, b_o, c_o = lax.map(block_p, (to_blocks(aq), to_blocks(iq), to_blocks(iw), to_blocks(q_lat),
                                          to_blocks(q_pe), to_blocks(cq), pos_p.reshape(-1, QBLOCK)))
        hp = mix_and_ffn(hp, from_blocks(a_o), from_blocks(b_o), from_blocks(c_o), lam_init, w_uv[l],
                         c_subln[l], w_out[l], norm_post_mix[l], norm_pre_ffn[l], w_gate[l], w_up[l],
                         w_down[l], norm_post_ffn[l])
        for i in range(7):
            new_p[i].append(rows_p[i])

        (aq, iq, iw, q_lat, q_pe, cq), rows_s = project(
            hs, pos_s, norm_pre_mix[l], w_in[l], q_a_norm[l], w_q_b[l], kv_a_norm[l], w_uk[l])
        ak_n, av_n, ik_n, b_lat_n, b_kpe_n, ck_n, cv_n = rows_s

        def paged(pool):
            g = pool[l, page_table]
            return g.reshape((g.shape[0], -1) + g.shape[3:])

        def gather_s(idx):
            pidx = jnp.minimum(idx, PAST_LEN - 1)
            phys = page_table[bidx_s, pidx // PAGE_SIZE]
            off = pidx % PAGE_SIZE
            nidx = jnp.clip(idx - PAST_LEN, 0, DEC_SEQ - 1)
            is_past = (idx < PAST_LEN)[..., None]
            k_sel = jnp.where(is_past, cache_a_k[l, phys, off], ak_n[bidx_s, nidx])
            v_sel = jnp.where(is_past, cache_a_v[l, phys, off], av_n[bidx_s, nidx])
            return k_sel, v_sel

        kidx_all = jnp.concatenate([paged(cache_a_kidx), ik_n], axis=1)
        lat_all = jnp.concatenate([paged(cache_b_latent), b_lat_n], axis=1)
        kpe_all = jnp.concatenate([paged(cache_b_krope), b_kpe_n], axis=1)
        ck_all = jnp.concatenate([paged(cache_c_k), ck_n], axis=1)
        cv_all = jnp.concatenate([paged(cache_c_v), cv_n], axis=1)
        a_s = dsa_attend(aq, iq, iw, pos_s, kidx_all, kpos_s, gather_s, topk_s, bias_a)
        b_s = mla_attend(q_lat, q_pe, pos_s, lat_all, kpe_all, kpos_s)
        c_s = diff_attend(cq, pos_s, ck_all, cv_all, kpos_s, lam, bias_c)
        hs = mix_and_ffn(hs, a_s, b_s, c_s, lam_init, w_uv[l], c_subln[l], w_out[l], norm_post_mix[l],
                         norm_pre_ffn[l], w_gate[l], w_up[l], w_down[l], norm_post_ffn[l])
        for i in range(7):
            new_s[i].append(rows_s[i])

    new_a_k_p = jnp.stack(new_p[0])
    new_a_v_p = jnp.stack(new_p[1])
    new_a_kidx_p = jnp.stack(new_p[2])
    new_b_latent_p = jnp.stack(new_p[3])
    new_b_krope_p = jnp.stack(new_p[4])
    new_c_k_p = jnp.stack(new_p[5])
    new_c_v_p = jnp.stack(new_p[6])
    new_a_k_s = jnp.stack(new_s[0])
    new_a_v_s = jnp.stack(new_s[1])
    new_a_kidx_s = jnp.stack(new_s[2])
    new_b_latent_s = jnp.stack(new_s[3])
    new_b_krope_s = jnp.stack(new_s[4])
    new_c_k_s = jnp.stack(new_s[5])
    new_c_v_s = jnp.stack(new_s[6])
    return (hp, hs, new_a_k_p, new_a_v_p, new_a_kidx_p, new_b_latent_p, new_b_krope_p, new_c_k_p,
            new_c_v_p, new_a_k_s, new_a_v_s, new_a_kidx_s, new_b_latent_s, new_b_krope_s, new_c_k_s,
            new_c_v_s)
```

```python
import functools
import math

import numpy as np
import jax
import jax.numpy as jnp
from jax import lax
from jax.experimental import pallas as pl
from jax.experimental.pallas import tpu as pltpu

F32 = jnp.float32
BF16 = jnp.bfloat16
I32 = jnp.int32

D_MODEL = 2048
HEAD_DIM = 128
HA = 4
HC = 4
HB = 8
DA = 128
H_IDX = 8
D_IDX = 64
TOPK_MAX = 256
Q_LORA = 512
KV_LORA = 256
DN = 128
DR = 64
DV = 128
HC_KV = 2
C_GROUP = 2
DC_QK = 64
DC_V = 128
N_BUCKETS = 32
MAX_DISTANCE = 128
ROPE_THETA = 10000.0
EPS = 1e-6
PAGE = 128

LANE = 128
NEG = -1e30
INT_MIN = -2 ** 31
VMEM_LIMIT = 52 * 1024 * 1024

C_AQ, C_IQ, C_BQ, C_CQ = 0, 512, 1024, 1536
C_LAT, C_CK, C_CV = 2048, 2304, 2560
C_AK, C_AV, C_IK2, C_KPE2, C_IW = 2816, 2944, 3072, 3200, 3328
IN_COLS_R = 3456
_O_AQ, _O_AK, _O_AV, _O_IQ, _O_IK, _O_IW, _O_BQ, _O_BKV, _O_CQ, _O_CK, _O_CV = (
    0, 512, 640, 768, 1280, 1344, 1352, 1864, 2184, 2696, 2952)


def _in_perm():
    perm = np.full((IN_COLS_R,), -1, np.int64)
    def put(dst, src, n):
        perm[dst:dst + n] = np.arange(src, src + n)
    put(C_AQ, _O_AQ, 512)
    put(C_IQ, _O_IQ, 512)
    put(C_BQ, _O_BQ, 512)
    put(C_CQ, _O_CQ, 512)
    put(C_LAT, _O_BKV, 256)
    put(C_CK, _O_CK, 256)
    put(C_CV, _O_CV, 256)
    put(C_AK, _O_AK, 128)
    put(C_AV, _O_AV, 128)
    put(C_IK2, _O_IK, 64)
    put(C_IK2 + 64, _O_IK, 64)
    kpe = _O_BKV + KV_LORA
    put(C_KPE2, kpe, 64)
    put(C_KPE2 + 64, kpe + 32, 32)
    put(C_KPE2 + 96, kpe, 32)
    put(C_IW, _O_IW, 8)
    return perm


def _qb_perm():
    nope, pe, pes = [], [], []
    for h in range(HB):
        base = h * (DN + DR)
        nope += list(range(base, base + DN))
        pe += list(range(base + DN, base + DN + DR))
        pes += list(range(base + DN + 32, base + DN + 64)) + list(range(base + DN, base + DN + 32))
    return np.array(nope + pe + pes, np.int64)


def _dot(a, b):
    return jnp.dot(a, b, preferred_element_type=F32)


def _dot_nt(a, b):
    return lax.dot_general(a, b, (((1,), (1,)), ((), ())), preferred_element_type=F32)


def _rms(x, g):
    return (x * lax.rsqrt(jnp.mean(x * x, axis=-1, keepdims=True) + EPS)) * g


def _cparams(sem):
    return pltpu.CompilerParams(dimension_semantics=sem, vmem_limit_bytes=VMEM_LIMIT)


def _row_tile(rows, want):
    t = min(rows, want)
    assert rows % t == 0, (rows, t)
    return t


def _norm_matmul_kernel(x_ref, g_ref, w_ref, o_ref, xn_ref):
    @pl.when(pl.program_id(1) == 0)
    def _():
        xn_ref[...] = _rms(x_ref[...], g_ref[...]).astype(BF16)
    o_ref[...] = _dot(xn_ref[...], w_ref[...])


def norm_matmul(x, g, w, tn):
    m, k = x.shape
    n = w.shape[1]
    tm = _row_tile(m, 512)
    return pl.pallas_call(
        _norm_matmul_kernel,
        grid=(m // tm, n // tn),
        in_specs=[pl.BlockSpec((tm, k), lambda i, j: (i, 0)),
                  pl.BlockSpec((1, k), lambda i, j: (0, 0)),
                  pl.BlockSpec((k, tn), lambda i, j: (0, j))],
        out_specs=pl.BlockSpec((tm, tn), lambda i, j: (i, j)),
        out_shape=jax.ShapeDtypeStruct((m, n), F32),
        scratch_shapes=[pltpu.VMEM((tm, k), BF16)],
        compiler_params=_cparams(("parallel", "arbitrary")),
        name="in_proj",
    )(x, g.reshape(1, k), w)


def _post_kernel(bq_ref, lat_ref, kpe2_ref, cos_ref, sin_ref, gq_ref, gkv_ref, wqb_ref, wuk_ref,
                 qlat_ref, qpe_ref, blat_ref, bkpe_ref):
    qn = _rms(bq_ref[...], gq_ref[...]).astype(BF16)
    qb = _dot(qn, wqb_ref[...])
    cos = cos_ref[...]
    sin = sin_ref[...]
    for h in range(HB):
        qlat_ref[h] = _dot(qb[:, h * DN:(h + 1) * DN].astype(BF16), wuk_ref[h])
    nq = HB * DN
    pe = qb[:, nq:nq + HB * DR] * cos + qb[:, nq + HB * DR:] * sin
    for h in range(HB):
        qpe_ref[h] = pe[:, h * DR:(h + 1) * DR]
    blat_ref[...] = _rms(lat_ref[...], gkv_ref[...])
    k2 = kpe2_ref[...]
    bkpe_ref[...] = k2[:, :DR] * cos[:, :DR] + k2[:, DR:] * sin[:, :DR]


def post_project(h, cos_t, sin_t, g_qa, g_kva, wqb, wuk):
    rows = h.shape[0]
    tm = _row_tile(rows, 512)
    nt = cos_t.shape[0] // tm
    return pl.pallas_call(
        _post_kernel,
        grid=(rows // tm,),
        in_specs=[pl.BlockSpec((tm, Q_LORA), lambda i: (i, C_BQ // Q_LORA)),
                  pl.BlockSpec((tm, KV_LORA), lambda i: (i, C_LAT // KV_LORA)),
                  pl.BlockSpec((tm, LANE), lambda i: (i, C_KPE2 // LANE)),
                  pl.BlockSpec((tm, HB * DR), lambda i: (i % nt, 0)),
                  pl.BlockSpec((tm, HB * DR), lambda i: (i % nt, 0)),
                  pl.BlockSpec((1, Q_LORA), lambda i: (0, 0)),
                  pl.BlockSpec((1, KV_LORA), lambda i: (0, 0)),
                  pl.BlockSpec(wqb.shape, lambda i: (0, 0)),
                  pl.BlockSpec(wuk.shape, lambda i: (0, 0, 0))],
        out_specs=[pl.BlockSpec((HB, tm, KV_LORA), lambda i: (0, i, 0)),
                   pl.BlockSpec((HB, tm, DR), lambda i: (0, i, 0)),
                   pl.BlockSpec((tm, KV_LORA), lambda i: (i, 0)),
                   pl.BlockSpec((tm, DR), lambda i: (i, 0))],
        out_shape=[jax.ShapeDtypeStruct((HB, rows, KV_LORA), F32),
                   jax.ShapeDtypeStruct((HB, rows, DR), F32),
                   jax.ShapeDtypeStruct((rows, KV_LORA), F32),
                   jax.ShapeDtypeStruct((rows, DR), F32)],
        compiler_params=_cparams(("parallel",)),
        name="post_proj",
    )(h, h, h, cos_t, sin_t, g_qa.reshape(1, -1), g_kva.reshape(1, -1), wqb, wuk)


def _bucket_np(n):
    n = np.maximum(n, 0)
    half = N_BUCKETS // 2
    nf = np.maximum(n, 1).astype(np.float32)
    large = half + (np.log(nf / np.float32(half)) / np.float32(math.log(MAX_DISTANCE / half))
                    * np.float32(N_BUCKETS - half)).astype(np.int32)
    return np.where(n < half, n, np.minimum(large, N_BUCKETS - 1)).astype(np.int32)


def _t5_kernel(tab_ref, bkt_ref, o_ref):
    bkt = bkt_ref[...]
    for h in range(HA + HC):
        acc = jnp.zeros(bkt.shape, F32)
        for b in range(N_BUCKETS):
            acc = jnp.where(bkt == b, tab_ref[b, h], acc)
        o_ref[h] = acc


def t5_expand(table, buckets):
    r, c = buckets.shape
    return pl.pallas_call(
        _t5_kernel,
        in_specs=[pl.BlockSpec(memory_space=pltpu.SMEM),
                  pl.BlockSpec((r, c), lambda: (0, 0))],
        out_specs=pl.BlockSpec((HA + HC, r, c), lambda: (0, 0, 0)),
        out_shape=jax.ShapeDtypeStruct((HA + HC, r, c), F32),
        name="t5_bias",
    )(table, buckets)


def _flash_update(s, v, m_ref, l_ref, acc_ref):
    m = m_ref[...]
    m_new = jnp.maximum(m, jnp.max(s, axis=-1, keepdims=True))
    alpha = jnp.exp(m - m_new)
    p = jnp.exp(s - m_new)
    l_ref[...] = alpha * l_ref[...] + jnp.sum(p, axis=-1, keepdims=True)
    acc_ref[...] = alpha * acc_ref[...] + _dot(p.astype(BF16), v)
    m_ref[...] = m_new


def _flash_init(m_ref, l_ref, acc_ref):
    m_ref[...] = jnp.full(m_ref.shape, NEG, F32)
    l_ref[...] = jnp.zeros(l_ref.shape, F32)
    acc_ref[...] = jnp.zeros(acc_ref.shape, F32)


def _count(mask, axes):
    c = jnp.where(mask, 1.0, 0.0)
    for ax in axes:
        c = jnp.sum(c, axis=ax, keepdims=True)
    return c


def _select_topk(sc, idx, topk, axes, nbits):
    sc = jnp.where(sc == 0.0, 0.0, sc)
    bits = pltpu.bitcast(sc, I32)
    key = bits ^ ((bits >> 31) & 0x7FFFFFFF)
    kf = float(topk)
    tshape = tuple(1 if a in axes else d for a, d in enumerate(sc.shape))

    def tbody(i, t):
        cand = t + jnp.left_shift(jnp.int32(1), 31 - i)
        return jnp.where(_count(key >= cand, axes) >= kf, cand, t)

    t = lax.fori_loop(0, 32, tbody, jnp.full(tshape, INT_MIN, I32))
    gt = key > t
    need = kf - _count(gt, axes)
    eq = key == t

    def jbody(i, j):
        cand = j + jnp.left_shift(jnp.int32(1), nbits - 1 - i)
        c = jnp.where(eq, jnp.where(idx < cand, 1.0, 0.0), 0.0)
        for ax in axes:
            c = jnp.sum(c, axis=ax, keepdims=True)
        return jnp.where(c < need, cand, j)

    jmax = lax.fori_loop(0, nbits, jbody, jnp.zeros(tshape, I32))
    take = jnp.where(gt, 1.0, jnp.where(eq, jnp.where(idx <= jmax, 1.0, 0.0), 0.0))
    take = jnp.where(sc > -jnp.inf, take, 0.0)
    return jnp.where(take > 0.5, 0.0, NEG)


def _zone(j, qi):
    return jnp.clip(j - qi + 2, 0, 2)


def _mla_p_kernel(qlat_ref, qpe_ref, klat_ref, kpe_ref, wuv_ref, o_ref, m_ref, l_ref, acc_ref, *, tq, scale):
    qi = pl.program_id(1)
    ql = qlat_ref[...].reshape(HB * tq, KV_LORA).astype(BF16)
    qp = qpe_ref[...].reshape(HB * tq, DR).astype(BF16)
    qpos = qi * tq + lax.broadcasted_iota(I32, (1, tq, 1), 1)
    lane = lax.broadcasted_iota(I32, (1, 1, tq), 2)
    _flash_init(m_ref, l_ref, acc_ref)

    def body(j, carry):
        k0 = pl.multiple_of(j * tq, tq)
        kl = klat_ref[pl.ds(k0, tq), :].astype(BF16)
        kp = kpe_ref[pl.ds(k0, tq), :].astype(BF16)
        s = ((_dot_nt(ql, kl) + _dot_nt(qp, kp)) * scale).reshape(HB, tq, tq)
        s = jnp.where(k0 + lane <= qpos, s, NEG).reshape(HB * tq, tq)
        _flash_update(s, kl, m_ref, l_ref, acc_ref)
        return carry

    lax.fori_loop(0, qi + 1, body, 0)
    o = acc_ref[...] / l_ref[...]
    for h in range(HB):
        o_ref[:, h * DV:(h + 1) * DV] = _dot(o[h * tq:(h + 1) * tq].astype(BF16), wuv_ref[h])


def mla_prompt(qlat, qpe, blat, bkpe, wuv, nb, seq, tq):
    nq = seq // tq
    rows = HB * tq
    return pl.pallas_call(
        functools.partial(_mla_p_kernel, tq=tq, scale=(DN + DR) ** -0.5),
        grid=(nb, nq),
        in_specs=[pl.BlockSpec((HB, tq, KV_LORA), lambda b, i: (0, b * nq + i, 0)),
                  pl.BlockSpec((HB, tq, DR), lambda b, i: (0, b * nq + i, 0)),
                  pl.BlockSpec((seq, KV_LORA), lambda b, i: (b, 0)),
                  pl.BlockSpec((seq, DR), lambda b, i: (b, 0)),
                  pl.BlockSpec(wuv.shape, lambda b, i: (0, 0, 0))],
        out_specs=pl.BlockSpec((tq, HB * DV), lambda b, i: (b * nq + i, 0)),
        out_shape=jax.ShapeDtypeStruct((nb * seq, HB * DV), F32),
        scratch_shapes=[pltpu.VMEM((rows, 1), F32), pltpu.VMEM((rows, 1), F32),
                        pltpu.VMEM((rows, KV_LORA), F32)],
        compiler_params=_cparams(("parallel", "arbitrary")),
        name="mla_prompt",
    )(qlat, qpe, blat, bkpe, wuv)


def _diff_lambda(lq1_ref, lk1_ref, lq2_ref, lk2_ref, lam_init):
    a = jnp.sum(lq1_ref[...] * lk1_ref[...], axis=-1, keepdims=True)
    b = jnp.sum(lq2_ref[...] * lk2_ref[...], axis=-1, keepdims=True)
    return jnp.exp(a) - jnp.exp(b) + lam_init


def _split_maps(x):
    lo = lax.broadcasted_iota(I32, (1, LANE), 1) < DC_QK
    return jnp.where(lo, x, 0.0), jnp.where(lo, 0.0, x)


def _diff_p_kernel(cq_ref, ck_ref, cv_ref, bias_ref, lq1_ref, lk1_ref, lq2_ref, lk2_ref, gsub_ref,
                   o_ref, m_ref, l_ref, acc_ref, *, tq, lam_init):
    qi = pl.program_id(1)
    lam = _diff_lambda(lq1_ref, lk1_ref, lq2_ref, lk2_ref, lam_init)
    x = cq_ref[...]
    qpos = qi * tq + lax.broadcasted_iota(I32, (1, tq, 1), 1)
    lane = lax.broadcasted_iota(I32, (1, 1, tq), 2)
    ng = 2 * C_GROUP
    for kvh in range(HC_KV):
        parts = []
        for g in range(C_GROUP):
            hc = kvh * C_GROUP + g
            parts += list(_split_maps(x[:, hc * LANE:(hc + 1) * LANE]))
        q = jnp.concatenate(parts, axis=0).astype(BF16)
        _flash_init(m_ref, l_ref, acc_ref)

        def body(j, carry, kvh=kvh, q=q):
            k0 = pl.multiple_of(j * tq, tq)
            k = ck_ref[pl.ds(k0, tq), kvh * LANE:(kvh + 1) * LANE].astype(BF16)
            v = cv_ref[pl.ds(k0, tq), kvh * DC_V:(kvh + 1) * DC_V].astype(BF16)
            s = (_dot_nt(q, k) * (DC_QK ** -0.5)).reshape(ng, tq, tq) + bias_ref[_zone(j, qi), kvh]
            s = jnp.where(k0 + lane <= qpos, s, NEG).reshape(ng * tq, tq)
            _flash_update(s, v, m_ref, l_ref, acc_ref)
            return carry

        lax.fori_loop(0, qi + 1, body, 0)
        o = acc_ref[...] / l_ref[...]
        for g in range(C_GROUP):
            d = o[(2 * g) * tq:(2 * g + 1) * tq] - lam * o[(2 * g + 1) * tq:(2 * g + 2) * tq]
            hc = kvh * C_GROUP + g
            o_ref[:, hc * DC_V:(hc + 1) * DC_V] = _rms(d, gsub_ref[...]) * (1.0 - lam_init)


def diff_prompt(h, bias, lams, gsub, nb, seq, tq, lam_init):
    nq = seq // tq
    rows = 2 * C_GROUP * tq
    vec = pl.BlockSpec((1, DC_QK), lambda b, i: (0, 0))
    return pl.pallas_call(
        functools.partial(_diff_p_kernel, tq=tq, lam_init=lam_init),
        grid=(nb, nq),
        in_specs=[pl.BlockSpec((tq, HC * LANE), lambda b, i: (b * nq + i, C_CQ // (HC * LANE))),
                  pl.BlockSpec((seq, HC_KV * LANE), lambda b, i: (b, C_CK // (HC_KV * LANE))),
                  pl.BlockSpec((seq, HC_KV * DC_V), lambda b, i: (b, C_CV // (HC_KV * DC_V))),
                  pl.BlockSpec(bias.shape, lambda b, i: (0, 0, 0, 0, 0)),
                  vec, vec, vec, vec,
                  pl.BlockSpec((1, DC_V), lambda b, i: (0, 0))],
        out_specs=pl.BlockSpec((tq, HC * DC_V), lambda b, i: (b * nq + i, 0)),
        out_shape=jax.ShapeDtypeStruct((nb * seq, HC * DC_V), F32),
        scratch_shapes=[pltpu.VMEM((rows, 1), F32), pltpu.VMEM((rows, 1), F32),
                        pltpu.VMEM((rows, DC_V), F32)],
        compiler_params=_cparams(("parallel", "arbitrary")),
        name="diff_prompt",
    )(h, h, h, bias, *lams, gsub.reshape(1, -1))


def _dsa_p_kernel(aq_ref, iq_ref, iw_ref, ik2_ref, ak_ref, av_ref, bias_ref, o_ref,
                  sc_ref, sb_ref, m_ref, l_ref, acc_ref, *, tq, nc, topk):
    qi = pl.program_id(1)
    iq = iq_ref[...]
    iw = iw_ref[...][:, :H_IDX] * (D_IDX ** -0.5 * H_IDX ** -0.5)
    parts = []
    for p in range(H_IDX // 2):
        parts += list(_split_maps(iq[:, p * LANE:(p + 1) * LANE]))
    xq = jnp.concatenate(parts, axis=0).astype(BF16)
    qpos = qi * tq + lax.broadcasted_iota(I32, (tq, 1), 0)
    lane = lax.broadcasted_iota(I32, (1, LANE), 1)
    sc_ref[...] = jnp.full(sc_ref.shape, -jnp.inf, F32)

    def sbody(j, carry):
        k0 = pl.multiple_of(j * LANE, LANE)
        k2 = ik2_ref[pl.ds(k0, LANE), :].astype(BF16)
        r = jnp.maximum(_dot_nt(xq, k2), 0.0).reshape(H_IDX, tq, LANE)
        s = r[0] * iw[:, 0:1]
        for h in range(1, H_IDX):
            s = s + r[h] * iw[:, h:h + 1]
        sc_ref[j] = jnp.where(k0 + lane <= qpos, s, -jnp.inf)
        return carry

    lax.fori_loop(0, qi + 1, sbody, 0)

    idx = (lax.broadcasted_iota(I32, (nc, 1, LANE), 0) * LANE
           + lax.broadcasted_iota(I32, (nc, 1, LANE), 2))
    sb_ref[...] = _select_topk(sc_ref[...], idx, topk, (0, 2), max(1, (nc * LANE - 1).bit_length()))

    aq = aq_ref[...]
    q = jnp.concatenate([aq[:, h * DA:(h + 1) * DA] for h in range(HA)], axis=0).astype(BF16)
    _flash_init(m_ref, l_ref, acc_ref)

    def abody(j, carry):
        k0 = pl.multiple_of(j * LANE, LANE)
        k = ak_ref[pl.ds(k0, LANE), :].astype(BF16)
        v = av_ref[pl.ds(k0, LANE), :].astype(BF16)
        s = (_dot_nt(q, k) * (DA ** -0.5)).reshape(HA, tq, LANE) + bias_ref[_zone(j, qi)] + sb_ref[j][None]
        _flash_update(s.reshape(HA * tq, LANE), v, m_ref, l_ref, acc_ref)
        return carry

    lax.fori_loop(0, qi + 1, abody, 0)
    o = acc_ref[...] / l_ref[...]
    for h in range(HA):
        o_ref[:, h * DA:(h + 1) * DA] = o[h * tq:(h + 1) * tq]


def dsa_prompt(h, bias, nb, seq, topk):
    tq = LANE
    nq = seq // tq
    rows = HA * tq
    return pl.pallas_call(
        functools.partial(_dsa_p_kernel, tq=tq, nc=nq, topk=topk),
        grid=(nb, nq),
        in_specs=[pl.BlockSpec((tq, HA * DA), lambda b, i: (b * nq + i, C_AQ // (HA * DA))),
                  pl.BlockSpec((tq, H_IDX * D_IDX), lambda b, i: (b * nq + i, C_IQ // (H_IDX * D_IDX))),
                  pl.BlockSpec((tq, LANE), lambda b, i: (b * nq + i, C_IW // LANE)),
                  pl.BlockSpec((seq, LANE), lambda b, i: (b, C_IK2 // LANE)),
                  pl.BlockSpec((seq, DA), lambda b, i: (b, C_AK // DA)),
                  pl.BlockSpec((seq, DA), lambda b, i: (b, C_AV // DA)),
                  pl.BlockSpec(bias.shape, lambda b, i: (0, 0, 0, 0))],
        out_specs=pl.BlockSpec((tq, HA * DA), lambda b, i: (b * nq + i, 0)),
        out_shape=jax.ShapeDtypeStruct((nb * seq, HA * DA), F32),
        scratch_shapes=[pltpu.VMEM((nq, tq, LANE), F32), pltpu.VMEM((nq, tq, LANE), F32),
                        pltpu.VMEM((rows, 1), F32), pltpu.VMEM((rows, 1), F32),
                        pltpu.VMEM((rows, DA), F32)],
        compiler_params=_cparams(("parallel", "arbitrary")),
        name="dsa_prompt",
    )(h, h, h, h, h, h, bias)


def _page_copies(pt_ref, b, first_page, n_pages, layer, slot, pools, bufs, sem):
    cps = []
    for p in range(n_pages):
        page = pt_ref[b, first_page + p]
        for a, (pool, buf) in enumerate(zip(pools, bufs)):
            cps.append(pltpu.make_async_copy(pool.at[layer, page], buf.at[slot, pl.ds(p * PAGE, PAGE)],
                                             sem.at[a, slot]))
    return cps


def _paged_pipeline(pt_ref, step, n_steps, pages_per_step, steps_per_batch, layer, pools, bufs, sem):
    slot = step % 2

    def copies(st, sl):
        return _page_copies(pt_ref, st // steps_per_batch, (st % steps_per_batch) * pages_per_step,
                            pages_per_step, layer, sl, pools, bufs, sem)

    @pl.when(step == 0)
    def _():
        for cp in copies(step, slot):
            cp.start()

    @pl.when(step + 1 < n_steps)
    def _():
        for cp in copies(step + 1, 1 - slot):
            cp.start()

    for cp in copies(step, slot):
        cp.wait()
    return slot


def _pad_rows(x, rows):
    return jnp.concatenate([x, jnp.zeros((rows - x.shape[0], x.shape[1]), x.dtype)], axis=0)


def _new_key_mask(rows, nq):
    q = lax.broadcasted_iota(I32, (rows, 1), 0) % nq
    j = lax.broadcasted_iota(I32, (1, LANE), 1)
    return j <= q


def _mla_s_kernel(pt_ref, qlat_ref, qpe_ref, nlat_ref, nkpe_ref, wuv_ref, lat_hbm, kpe_hbm, o_ref,
                  latbuf, kpebuf, sem, m_ref, l_ref, acc_ref, *, layer, pps, nc, nb, nq, scale):
    b = pl.program_id(0)
    c = pl.program_id(1)
    slot = _paged_pipeline(pt_ref, b * nc + c, nb * nc, pps, nc, layer,
                           (lat_hbm, kpe_hbm), (latbuf, kpebuf), sem)
    rows = HB * nq
    ql = qlat_ref[...].reshape(rows, KV_LORA).astype(BF16)
    qp = qpe_ref[...].reshape(rows, DR).astype(BF16)

    @pl.when(c == 0)
    def _():
        _flash_init(m_ref, l_ref, acc_ref)

    kl = latbuf[slot].astype(BF16)
    kp = kpebuf[slot].astype(BF16)
    _flash_update((_dot_nt(ql, kl) + _dot_nt(qp, kp)) * scale, kl, m_ref, l_ref, acc_ref)

    @pl.when(c == nc - 1)
    def _():
        nl = _pad_rows(nlat_ref[...], LANE).astype(BF16)
        npe = _pad_rows(nkpe_ref[...], LANE).astype(BF16)
        s = (_dot_nt(ql, nl) + _dot_nt(qp, npe)) * scale
        _flash_update(jnp.where(_new_key_mask(rows, nq), s, NEG), nl, m_ref, l_ref, acc_ref)
        o = acc_ref[...] / l_ref[...]
        for h in range(HB):
            o_ref[:, h * DV:(h + 1) * DV] = _dot(o[h * nq:(h + 1) * nq].astype(BF16), wuv_ref[h])


def mla_sample(page_table, qlat, qpe, blat, bkpe, wuv, cache_lat, cache_kpe, layer, nq, pps):
    nb, n_pages = page_table.shape
    nc = n_pages // pps
    rows = HB * nq
    grid_spec = pltpu.PrefetchScalarGridSpec(
        num_scalar_prefetch=1,
        grid=(nb, nc),
        in_specs=[pl.BlockSpec((HB, nq, KV_LORA), lambda b, c, pt: (0, b, 0)),
                  pl.BlockSpec((HB, nq, DR), lambda b, c, pt: (0, b, 0)),
                  pl.BlockSpec((nq, KV_LORA), lambda b, c, pt: (b, 0)),
                  pl.BlockSpec((nq, DR), lambda b, c, pt: (b, 0)),
                  pl.BlockSpec(wuv.shape, lambda b, c, pt: (0, 0, 0)),
                  pl.BlockSpec(memory_space=pl.ANY),
                  pl.BlockSpec(memory_space=pl.ANY)],
        out_specs=pl.BlockSpec((nq, HB * DV), lambda b, c, pt: (b, 0)),
        scratch_shapes=[pltpu.VMEM((2, pps * PAGE, KV_LORA), F32),
                        pltpu.VMEM((2, pps * PAGE, DR), F32),
                        pltpu.SemaphoreType.DMA((2, 2)),
                        pltpu.VMEM((rows, 1), F32), pltpu.VMEM((rows, 1), F32),
                        pltpu.VMEM((rows, KV_LORA), F32)])
    return pl.pallas_call(
        functools.partial(_mla_s_kernel, layer=layer, pps=pps, nc=nc, nb=nb, nq=nq, scale=(DN + DR) ** -0.5),
        grid_spec=grid_spec,
        out_shape=jax.ShapeDtypeStruct((nb * nq, HB * DV), F32),
        compiler_params=_cparams(("arbitrary", "arbitrary")),
        name="mla_sample",
    )(page_table, qlat, qpe, blat, bkpe, wuv, cache_lat, cache_kpe)


def _diff_s_kernel(pt_ref, cq_ref, nck_ref, ncv_ref, bias_ref, bnew_ref, lq1_ref, lk1_ref, lq2_ref, lk2_ref,
                   gsub_ref, ck_hbm, cv_hbm, o_ref, ckbuf, cvbuf, sem, m_ref, l_ref, acc_ref,
                   *, layer, pps, nc, nb, nq, lam_init):
    b = pl.program_id(0)
    c = pl.program_id(1)
    slot = _paged_pipeline(pt_ref, b * nc + c, nb * nc, pps, nc, layer,
                           (ck_hbm, cv_hbm), (ckbuf, cvbuf), sem)
    x = cq_ref[...]
    zero = jnp.zeros((nq, LANE), F32)
    parts = []
    for kvh in range(HC_KV):
        for g in range(C_GROUP):
            hc = kvh * C_GROUP + g
            for piece in _split_maps(x[:, hc * LANE:(hc + 1) * LANE]):
                parts.append(jnp.concatenate([piece, zero] if kvh == 0 else [zero, piece], axis=1))
    q = jnp.concatenate(parts, axis=0).astype(BF16)

    @pl.when(c == 0)
    def _():
        _flash_init(m_ref, l_ref, acc_ref)

    k = ckbuf[slot].astype(BF16)
    v = cvbuf[slot].astype(BF16)
    _flash_update(_dot_nt(q, k) * (DC_QK ** -0.5) + bias_ref[0], v, m_ref, l_ref, acc_ref)

    @pl.when(c == nc - 1)
    def _():
        rows = 2 * HC * nq
        nk = _pad_rows(nck_ref[...], LANE).astype(BF16)
        nv = _pad_rows(ncv_ref[...], LANE).astype(BF16)
        s = _dot_nt(q, nk) * (DC_QK ** -0.5) + bnew_ref[...]
        _flash_update(jnp.where(_new_key_mask(rows, nq), s, NEG), nv, m_ref, l_ref, acc_ref)
        o = acc_ref[...] / l_ref[...]
        lam = _diff_lambda(lq1_ref, lk1_ref, lq2_ref, lk2_ref, lam_init)
        for kvh in range(HC_KV):
            for g in range(C_GROUP):
                hc = kvh * C_GROUP + g
                r0 = (hc * 2) * nq
                half = slice(kvh * DC_V, (kvh + 1) * DC_V)
                d = o[r0:r0 + nq, half] - lam * o[r0 + nq:r0 + 2 * nq, half]
                o_ref[:, hc * DC_V:(hc + 1) * DC_V] = _rms(d, gsub_ref[...]) * (1.0 - lam_init)


def diff_sample(page_table, h, bias, bias_new, lams, gsub, cache_ck, cache_cv, layer, nq, pps, lam_init):
    nb, n_pages = page_table.shape
    nc = n_pages // pps
    rows = 2 * HC * nq
    vec = pl.BlockSpec((1, DC_QK), lambda b, c, pt: (0, 0))
    grid_spec = pltpu.PrefetchScalarGridSpec(
        num_scalar_prefetch=1,
        grid=(nb, nc),
        in_specs=[pl.BlockSpec((nq, HC * LANE), lambda b, c, pt: (b, C_CQ // (HC * LANE))),
                  pl.BlockSpec((nq, HC_KV * LANE), lambda b, c, pt: (b, C_CK // (HC_KV * LANE))),
                  pl.BlockSpec((nq, HC_KV * DC_V), lambda b, c, pt: (b, C_CV // (HC_KV * DC_V))),
                  pl.BlockSpec((1, rows, pps * PAGE), lambda b, c, pt: (jnp.where(c == nc - 1, 1, 0), 0, 0)),
                  pl.BlockSpec((rows, LANE), lambda b, c, pt: (0, 0)),
                  vec, vec, vec, vec,
                  pl.BlockSpec((1, DC_V), lambda b, c, pt: (0, 0)),
                  pl.BlockSpec(memory_space=pl.ANY),
                  pl.BlockSpec(memory_space=pl.ANY)],
        out_specs=pl.BlockSpec((nq, HC * DC_V), lambda b, c, pt: (b, 0)),
        scratch_shapes=[pltpu.VMEM((2, pps * PAGE, HC_KV * LANE), F32),
                        pltpu.VMEM((2, pps * PAGE, HC_KV * DC_V), F32),
                        pltpu.SemaphoreType.DMA((2, 2)),
                        pltpu.VMEM((rows, 1), F32), pltpu.VMEM((rows, 1), F32),
                        pltpu.VMEM((rows, HC_KV * DC_V), F32)])
    return pl.pallas_call(
        functools.partial(_diff_s_kernel, layer=layer, pps=pps, nc=nc, nb=nb, nq=nq, lam_init=lam_init),
        grid_spec=grid_spec,
        out_shape=jax.ShapeDtypeStruct((nb * nq, HC * DC_V), F32),
        compiler_params=_cparams(("arbitrary", "arbitrary")),
        name="diff_sample",
    )(page_table, h, h, h, bias, bias_new, *lams, gsub.reshape(1, -1), cache_ck, cache_cv)


def _dsa_s_kernel(pt_ref, aq_ref, iq_ref, iw_ref, nik_ref, nak_ref, nav_ref, bias_ref, bnew_ref,
                  kidx_hbm, ak_hbm, av_hbm, o_ref, kidxbuf, akbuf, avbuf, sem, sc_ref, m_ref, l_ref, acc_ref,
                  *, layer, n_pages, nb, nq, topk, cw):
    b = pl.program_id(0)
    slot = _paged_pipeline(pt_ref, b, nb, n_pages, 1, layer,
                           (kidx_hbm, ak_hbm, av_hbm), (kidxbuf, akbuf, avbuf), sem)
    past = n_pages * PAGE
    iq = iq_ref[...]
    iw = iw_ref[...] * (D_IDX ** -0.5 * H_IDX ** -0.5)
    xq = jnp.concatenate([iq[:, h * D_IDX:(h + 1) * D_IDX] for h in range(H_IDX)], axis=0).astype(BF16)
    iwc = jnp.concatenate([iw[:, h:h + 1] for h in range(H_IDX)], axis=0)

    def scores(kc):
        r = jnp.maximum(_dot_nt(xq, kc), 0.0) * iwc
        return jnp.sum(r.reshape(H_IDX, nq, kc.shape[0]), axis=0)

    for cc in range(past // cw):
        sc_ref[:, cc * cw:(cc + 1) * cw] = scores(kidxbuf[slot, cc * cw:(cc + 1) * cw, :].astype(BF16))
    s_new = scores(_pad_rows(nik_ref[...][:, :D_IDX], LANE).astype(BF16))
    sc_ref[:, past:] = jnp.where(_new_key_mask(nq, nq), s_new, -jnp.inf)

    idx = lax.broadcasted_iota(I32, (1, past + LANE), 1)
    sb = _select_topk(sc_ref[...], idx, topk, (1,), (past + LANE - 1).bit_length())

    aq = aq_ref[...]
    q = jnp.concatenate([aq[:, h * DA:(h + 1) * DA] for h in range(HA)], axis=0).astype(BF16)
    rows = HA * nq
    _flash_init(m_ref, l_ref, acc_ref)

    def attend(k, v, bias, sbc):
        s = (_dot_nt(q, k) * (DA ** -0.5)).reshape(HA, nq, k.shape[0]) + bias + sbc[None]
        _flash_update(s.reshape(rows, k.shape[0]), v, m_ref, l_ref, acc_ref)

    for cc in range(past // cw):
        ks = slice(cc * cw, (cc + 1) * cw)
        attend(akbuf[slot, ks, :].astype(BF16), avbuf[slot, ks, :].astype(BF16), bias_ref[:, :, ks], sb[:, ks])
    attend(_pad_rows(nak_ref[...], LANE).astype(BF16), _pad_rows(nav_ref[...], LANE).astype(BF16),
           bnew_ref[...], sb[:, past:])
    o = acc_ref[...] / l_ref[...]
    for h in range(HA):
        o_ref[:, h * DA:(h + 1) * DA] = o[h * nq:(h + 1) * nq]


def dsa_sample(page_table, h, bias, bias_new, cache_kidx, cache_ak, cache_av, layer, nq, topk):
    nb, n_pages = page_table.shape
    past = n_pages * PAGE
    rows = HA * nq
    cw = min(past, 2048)
    grid_spec = pltpu.PrefetchScalarGridSpec(
        num_scalar_prefetch=1,
        grid=(nb,),
        in_specs=[pl.BlockSpec((nq, HA * DA), lambda b, pt: (b, C_AQ // (HA * DA))),
                  pl.BlockSpec((nq, H_IDX * D_IDX), lambda b, pt: (b, C_IQ // (H_IDX * D_IDX))),
                  pl.BlockSpec((nq, LANE), lambda b, pt: (b, C_IW // LANE)),
                  pl.BlockSpec((nq, LANE), lambda b, pt: (b, C_IK2 // LANE)),
                  pl.BlockSpec((nq, DA), lambda b, pt: (b, C_AK // DA)),
                  pl.BlockSpec((nq, DA), lambda b, pt: (b, C_AV // DA)),
                  pl.BlockSpec(bias.shape, lambda b, pt: (0, 0, 0)),
                  pl.BlockSpec(bias_new.shape, lambda b, pt: (0, 0, 0)),
                  pl.BlockSpec(memory_space=pl.ANY),
                  pl.BlockSpec(memory_space=pl.ANY),
                  pl.BlockSpec(memory_space=pl.ANY)],
        out_specs=pl.BlockSpec((nq, HA * DA), lambda b, pt: (b, 0)),
        scratch_shapes=[pltpu.VMEM((2, past, D_IDX), F32),
                        pltpu.VMEM((2, past, DA), F32),
                        pltpu.VMEM((2, past, DA), F32),
                        pltpu.SemaphoreType.DMA((3, 2)),
                        pltpu.VMEM((nq, past + LANE), F32),
                        pltpu.VMEM((rows, 1), F32), pltpu.VMEM((rows, 1), F32),
                        pltpu.VMEM((rows, DA), F32)])
    return pl.pallas_call(
        functools.partial(_dsa_s_kernel, layer=layer, n_pages=n_pages, nb=nb, nq=nq, topk=topk, cw=cw),
        grid_spec=grid_spec,
        out_shape=jax.ShapeDtypeStruct((nb * nq, HA * DA), F32),
        compiler_params=_cparams(("arbitrary",)),
        name="dsa_sample",
    )(page_table, h, h, h, h, h, h, bias, bias_new, cache_kidx, cache_ak, cache_av)


def _out_kernel(a_ref, b_ref, c_ref, x_ref, w_ref, g_ref, o_ref):
    na = HA * HEAD_DIM
    nab = na + HB * DV
    y = (_dot(a_ref[...].astype(BF16), w_ref[:na, :])
         + _dot(b_ref[...].astype(BF16), w_ref[na:nab, :])
         + _dot(c_ref[...].astype(BF16), w_ref[nab:, :]))
    o_ref[...] = x_ref[...] + _rms(y, g_ref[...])


def out_project(a, b, c, x, w, g):
    rows = x.shape[0]
    tm = _row_tile(rows, 256)
    row = lambda i: (i, 0)
    return pl.pallas_call(
        _out_kernel,
        grid=(rows // tm,),
        in_specs=[pl.BlockSpec((tm, a.shape[1]), row), pl.BlockSpec((tm, b.shape[1]), row),
                  pl.BlockSpec((tm, c.shape[1]), row), pl.BlockSpec((tm, D_MODEL), row),
                  pl.BlockSpec(w.shape, lambda i: (0, 0)), pl.BlockSpec((1, D_MODEL), lambda i: (0, 0))],
        out_specs=pl.BlockSpec((tm, D_MODEL), row),
        out_shape=jax.ShapeDtypeStruct((rows, D_MODEL), F32),
        compiler_params=_cparams(("parallel",)),
        name="out_proj",
    )(a, b, c, x, w, g.reshape(1, -1))


def _ffn_kernel(x_ref, gpre_ref, wg_ref, wu_ref, wd_ref, gpost_ref, o_ref, xn_ref, acc_ref):
    j = pl.program_id(1)

    @pl.when(j == 0)
    def _():
        xn_ref[...] = _rms(x_ref[...], gpre_ref[...]).astype(BF16)
        acc_ref[...] = jnp.zeros(acc_ref.shape, F32)

    xn = xn_ref[...]
    a = _dot(xn, wg_ref[...])
    u = _dot(xn, wu_ref[...])
    hmid = (a * (1.0 / (1.0 + jnp.exp(-a)))) * u
    acc_ref[...] += _dot(hmid.astype(BF16), wd_ref[...])

    @pl.when(j == pl.num_programs(1) - 1)
    def _():
        o_ref[...] = x_ref[...] + _rms(acc_ref[...], gpost_ref[...])


def ffn(x, g_pre, wg, wu, wd, g_post, tf):
    rows = x.shape[0]
    dff = wg.shape[1]
    tm = _row_tile(rows, 512)
    return pl.pallas_call(
        _ffn_kernel,
        grid=(rows // tm, dff // tf),
        in_specs=[pl.BlockSpec((tm, D_MODEL), lambda i, j: (i, 0)),
                  pl.BlockSpec((1, D_MODEL), lambda i, j: (0, 0)),
                  pl.BlockSpec((D_MODEL, tf), lambda i, j: (0, j)),
                  pl.BlockSpec((D_MODEL, tf), lambda i, j: (0, j)),
                  pl.BlockSpec((tf, D_MODEL), lambda i, j: (j, 0)),
                  pl.BlockSpec((1, D_MODEL), lambda i, j: (0, 0))],
        out_specs=pl.BlockSpec((tm, D_MODEL), lambda i, j: (i, 0)),
        out_shape=jax.ShapeDtypeStruct((rows, D_MODEL), F32),
        scratch_shapes=[pltpu.VMEM((tm, D_MODEL), BF16), pltpu.VMEM((tm, D_MODEL), F32)],
        compiler_params=_cparams(("parallel", "arbitrary")),
        name="ffn",
    )(x, g_pre.reshape(1, -1), wg, wu, wd, g_post.reshape(1, -1))


def _rope_tables(pos, rows):
    half = DR // 2
    inv = jnp.power(ROPE_THETA, -jnp.arange(half, dtype=F32) / half)
    ang = pos.astype(F32)[:, None] * inv[None, :]
    cos, sin = jnp.cos(ang), jnp.sin(ang)
    c = jnp.tile(jnp.concatenate([cos, cos], axis=-1), (rows // pos.shape[0], HB))
    s = jnp.tile(jnp.concatenate([-sin, sin], axis=-1), (rows // pos.shape[0], HB))
    return c, s


def _bias_buckets(tq, nq_s, past):
    i = np.arange(tq)[:, None]
    j = np.arange(LANE)[None, :]
    far = np.full((tq, LANE), MAX_DISTANCE, np.int64)
    zones = [far, i - j + tq, np.maximum(i - j, 0)]
    qs = np.arange(nq_s)[:, None]
    last = (past + qs) - (past - PAGE + j)
    new = np.maximum(qs - j, 0)
    return _bucket_np(np.concatenate(zones + [last, new], axis=0))


def kernel(x_prompt, x_sample, cache_a_k, cache_a_v, cache_a_kidx, cache_b_latent, cache_b_krope, cache_c_k, cache_c_v, page_table, rel_bias, norm_pre_mix, w_in, q_a_norm, w_q_b, kv_a_norm, w_uk, w_uv, lambda_q1, lambda_k1, lambda_q2, lambda_k2, c_subln, w_out, norm_post_mix, norm_pre_ffn, w_gate, w_up, w_down, norm_post_ffn):
    nb_p, seq, _ = x_prompt.shape
    nb_s, nq_s, _ = x_sample.shape
    depth = w_in.shape[0]
    n_pages = page_table.shape[1]
    past = n_pages * PAGE
    topk_p = min(TOPK_MAX, seq // 4)
    topk_s = min(TOPK_MAX, (past + nq_s) // 4)
    tq = LANE
    assert seq % tq == 0 and nq_s == 8 and past >= PAGE
    pps = min(n_pages, 16)
    assert n_pages % pps == 0

    perm = _in_perm()
    w_in_r = (jnp.take(w_in, jnp.asarray(np.maximum(perm, 0)), axis=2)
              * jnp.asarray(perm >= 0, F32)).astype(BF16)
    w_qb_r = jnp.take(w_q_b, jnp.asarray(_qb_perm()), axis=2).astype(BF16)
    w_uk_r = jnp.transpose(w_uk, (0, 2, 3, 1)).astype(BF16)
    w_uv_r = jnp.transpose(w_uv, (0, 2, 1, 3)).astype(BF16)
    w_out_b = w_out.astype(BF16)
    w_gate_b = w_gate.astype(BF16)
    w_up_b = w_up.astype(BF16)
    w_down_b = w_down.astype(BF16)

    rows_p = nb_p * seq
    rows_s = nb_s * nq_s
    tm_p = _row_tile(rows_p, 512)
    tm_s = _row_tile(rows_s, 512)
    cos_p, sin_p = _rope_tables(jnp.arange(seq, dtype=I32), max(seq, tm_p))
    cos_s, sin_s = _rope_tables(past + jnp.arange(nq_s, dtype=I32), max(nq_s, tm_s))

    tiles = t5_expand(rel_bias, jnp.asarray(_bias_buckets(tq, nq_s, past)))
    zones = tiles[:, :3 * tq].reshape(HA + HC, 3, tq, LANE)
    bias_a_p = jnp.transpose(zones[:HA], (1, 0, 2, 3))
    zc = jnp.transpose(zones[HA:], (1, 0, 2, 3)).reshape(3, HC_KV, C_GROUP, 1, tq, LANE)
    bias_c_p = jnp.broadcast_to(zc, (3, HC_KV, C_GROUP, 2, tq, LANE)).reshape(3, HC_KV, 2 * C_GROUP, tq, LANE)
    last = tiles[:, 3 * tq:3 * tq + nq_s]
    new = tiles[:, 3 * tq + nq_s:]
    far = jnp.broadcast_to(tiles[:, 0:1, 0:1], (HA + HC, nq_s, past - PAGE))
    full = jnp.concatenate([far, last], axis=-1)
    bias_a_s, bias_a_new = full[:HA], new[:HA]
    w = pps * PAGE
    rep = lambda t: jnp.repeat(t[HA:], 2, axis=0).reshape(2 * HC * nq_s, t.shape[-1])
    bias_c_s = jnp.stack([jnp.broadcast_to(rep(far)[:, :1], (2 * HC * nq_s, w)), rep(full)[:, past - w:]])
    bias_c_new = rep(new)

    cache_c_k2 = cache_c_k.reshape(cache_c_k.shape[:3] + (HC_KV * LANE,))
    cache_c_v2 = cache_c_v.reshape(cache_c_v.shape[:3] + (HC_KV * DC_V,))
    hp = x_prompt.reshape(rows_p, D_MODEL)
    hs = x_sample.reshape(rows_s, D_MODEL)
    new_p = [[] for _ in range(7)]
    new_s = [[] for _ in range(7)]
    for l in range(depth):
        lam_init = 0.8 - 0.6 * math.exp(-0.3 * l)
        lams = tuple(v[l].reshape(1, DC_QK) for v in (lambda_q1, lambda_k1, lambda_q2, lambda_k2))
        outs = []
        for grp, (x, cos_t, sin_t) in enumerate(((hp, cos_p, sin_p), (hs, cos_s, sin_s))):
            h = norm_matmul(x, norm_pre_mix[l], w_in_r[l], IN_COLS_R // 3)
            qlat, qpe, blat, bkpe = post_project(h, cos_t, sin_t, q_a_norm[l], kv_a_norm[l], w_qb_r[l], w_uk_r[l])
            if grp == 0:
                a_o = dsa_prompt(h, bias_a_p, nb_p, seq, topk_p)
                b_o = mla_prompt(qlat, qpe, blat, bkpe, w_uv_r[l], nb_p, seq, tq)
                c_o = diff_prompt(h, bias_c_p, lams, c_subln[l], nb_p, seq, tq, lam_init)
            else:
                a_o = dsa_sample(page_table, h, bias_a_s, bias_a_new, cache_a_kidx, cache_a_k, cache_a_v,
                                 l, nq_s, topk_s)
                b_o = mla_sample(page_table, qlat, qpe, blat, bkpe, w_uv_r[l], cache_b_latent, cache_b_krope,
                                 l, nq_s, pps)
                c_o = diff_sample(page_table, h, bias_c_s, bias_c_new, lams, c_subln[l], cache_c_k2, cache_c_v2,
                                  l, nq_s, pps, lam_init)
            x1 = out_project(a_o, b_o, c_o, x, w_out_b[l], norm_post_mix[l])
            outs.append(ffn(x1, norm_pre_ffn[l], w_gate_b[l], w_up_b[l], w_down_b[l], norm_post_ffn[l], 512))
            rows = (h[:, C_AK:C_AK + DA], h[:, C_AV:C_AV + DA], h[:, C_IK2:C_IK2 + D_IDX], blat, bkpe,
                    h[:, C_CK:C_CK + HC_KV * LANE], h[:, C_CV:C_CV + HC_KV * DC_V])
            for i in range(7):
                (new_p if grp == 0 else new_s)[i].append(rows[i])
        hp, hs = outs

    def stack(parts, lead, tail):
        return jnp.stack(parts).reshape((depth,) + lead + tail)

    tails = ((DA,), (DA,), (D_IDX,), (KV_LORA,), (DR,), (HC_KV, 2 * DC_QK), (HC_KV, DC_V))
    outs_p = tuple(stack(new_p[i], (nb_p, seq), tails[i]) for i in range(7))
    outs_s = tuple(stack(new_s[i], (nb_s, nq_s), tails[i]) for i in range(7))
    return (hp.reshape(x_prompt.shape), hs.reshape(x_sample.shape)) + outs_p + outs_s
```

```python
import functools
import math

import numpy as np
import jax
import jax.numpy as jnp
from jax import lax
from jax.experimental import pallas as pl
from jax.experimental.pallas import tpu as pltpu

F32 = jnp.float32
BF16 = jnp.bfloat16
I32 = jnp.int32

D_MODEL = 2048
HEAD_DIM = 128
HA = 4
HC = 4
HB = 8
DA = 128
H_IDX = 8
D_IDX = 64
TOPK_MAX = 256
Q_LORA = 512
KV_LORA = 256
DN = 128
DR = 64
DV = 128
HC_KV = 2
C_GROUP = 2
DC_QK = 64
DC_V = 128
N_BUCKETS = 32
MAX_DISTANCE = 128
ROPE_THETA = 10000.0
EPS = 1e-6
PAGE = 128

LANE = 128
NEG = -1e30
INT_MIN = -2 ** 31
INT_MAX = 2 ** 31 - 1
NINF_KEY = -2139095041
VMEM_LIMIT = 52 * 1024 * 1024
TQ = 128
TK = 512
CW = 2048
MLA_SPLIT = 2

C_AQ, C_IQ, C_BQ, C_CQ = 0, 512, 1024, 1536
C_LAT, C_CK, C_CV = 2048, 2304, 2560
C_AK, C_AV, C_IK2, C_KPE2, C_IW = 2816, 2944, 3072, 3200, 3328
IN_COLS_R = 3456
_O_AQ, _O_AK, _O_AV, _O_IQ, _O_IK, _O_IW, _O_BQ, _O_BKV, _O_CQ, _O_CK, _O_CV = (
    0, 512, 640, 768, 1280, 1344, 1352, 1864, 2184, 2696, 2952)


def _in_perm():
    perm = np.full((IN_COLS_R,), -1, np.int64)
    def put(dst, src, n):
        perm[dst:dst + n] = np.arange(src, src + n)
    put(C_AQ, _O_AQ, 512)
    put(C_IQ, _O_IQ, 512)
    put(C_BQ, _O_BQ, 512)
    put(C_CQ, _O_CQ, 512)
    put(C_LAT, _O_BKV, 256)
    put(C_CK, _O_CK, 256)
    put(C_CV, _O_CV, 256)
    put(C_AK, _O_AK, 128)
    put(C_AV, _O_AV, 128)
    put(C_IK2, _O_IK, 64)
    put(C_IK2 + 64, _O_IK, 64)
    kpe = _O_BKV + KV_LORA
    put(C_KPE2, kpe, 64)
    put(C_KPE2 + 64, kpe + 32, 32)
    put(C_KPE2 + 96, kpe, 32)
    put(C_IW, _O_IW, 8)
    return perm


def _qb_perm():
    nope, pe, pes = [], [], []
    for h in range(HB):
        base = h * (DN + DR)
        nope += list(range(base, base + DN))
        pe += list(range(base + DN, base + DN + DR))
        pes += list(range(base + DN + 32, base + DN + 64)) + list(range(base + DN, base + DN + 32))
    return np.array(nope + pe + pes, np.int64)


def _dot(a, b):
    return jnp.dot(a, b, preferred_element_type=F32)


def _dot_nt(a, b):
    return lax.dot_general(a, b, (((1,), (1,)), ((), ())), preferred_element_type=F32)


def _rms(x, g):
    return (x * lax.rsqrt(jnp.mean(x * x, axis=-1, keepdims=True) + EPS)) * g


def _cparams(sem):
    return pltpu.CompilerParams(dimension_semantics=sem, vmem_limit_bytes=VMEM_LIMIT)


def _row_tile(rows, want):
    t = min(rows, want)
    assert rows % t == 0, (rows, t)
    return t


def _norm_matmul_kernel(x_ref, g_ref, w_ref, o_ref, xn_ref):
    @pl.when(pl.program_id(1) == 0)
    def _():
        xn_ref[...] = _rms(x_ref[...], g_ref[...]).astype(BF16)
    o_ref[...] = _dot(xn_ref[...], w_ref[...])


def norm_matmul(x, g, w, tn):
    m, k = x.shape
    n = w.shape[1]
    tm = _row_tile(m, 512)
    return pl.pallas_call(
        _norm_matmul_kernel,
        grid=(m // tm, n // tn),
        in_specs=[pl.BlockSpec((tm, k), lambda i, j: (i, 0)),
                  pl.BlockSpec((1, k), lambda i, j: (0, 0)),
                  pl.BlockSpec((k, tn), lambda i, j: (0, j))],
        out_specs=pl.BlockSpec((tm, tn), lambda i, j: (i, j)),
        out_shape=jax.ShapeDtypeStruct((m, n), F32),
        scratch_shapes=[pltpu.VMEM((tm, k), BF16)],
        compiler_params=_cparams(("parallel", "arbitrary")),
        name="in_proj",
    )(x, g.reshape(1, k), w)


def _post_kernel(bq_ref, lat_ref, kpe2_ref, cos_ref, sin_ref, gq_ref, gkv_ref, wqb_ref, wuk_ref,
                 qlat_ref, qpe_ref, blat_ref, bkpe_ref):
    qn = _rms(bq_ref[...], gq_ref[...]).astype(BF16)
    qb = _dot(qn, wqb_ref[...])
    cos = cos_ref[...]
    sin = sin_ref[...]
    for h in range(HB):
        qlat_ref[h] = _dot(qb[:, h * DN:(h + 1) * DN].astype(BF16), wuk_ref[h])
    nq = HB * DN
    pe = qb[:, nq:nq + HB * DR] * cos + qb[:, nq + HB * DR:] * sin
    for h in range(HB):
        qpe_ref[h] = pe[:, h * DR:(h + 1) * DR]
    blat_ref[...] = _rms(lat_ref[...], gkv_ref[...])
    k2 = kpe2_ref[...]
    bkpe_ref[...] = k2[:, :DR] * cos[:, :DR] + k2[:, DR:] * sin[:, :DR]


def post_project(h, cos_t, sin_t, g_qa, g_kva, wqb, wuk):
    rows = h.shape[0]
    tm = _row_tile(rows, 512)
    nt = cos_t.shape[0] // tm
    return pl.pallas_call(
        _post_kernel,
        grid=(rows // tm,),
        in_specs=[pl.BlockSpec((tm, Q_LORA), lambda i: (i, C_BQ // Q_LORA)),
                  pl.BlockSpec((tm, KV_LORA), lambda i: (i, C_LAT // KV_LORA)),
                  pl.BlockSpec((tm, LANE), lambda i: (i, C_KPE2 // LANE)),
                  pl.BlockSpec((tm, HB * DR), lambda i: (i % nt, 0)),
                  pl.BlockSpec((tm, HB * DR), lambda i: (i % nt, 0)),
                  pl.BlockSpec((1, Q_LORA), lambda i: (0, 0)),
                  pl.BlockSpec((1, KV_LORA), lambda i: (0, 0)),
                  pl.BlockSpec(wqb.shape, lambda i: (0, 0)),
                  pl.BlockSpec(wuk.shape, lambda i: (0, 0, 0))],
        out_specs=[pl.BlockSpec((HB, tm, KV_LORA), lambda i: (0, i, 0)),
                   pl.BlockSpec((HB, tm, DR), lambda i: (0, i, 0)),
                   pl.BlockSpec((tm, KV_LORA), lambda i: (i, 0)),
                   pl.BlockSpec((tm, DR), lambda i: (i, 0))],
        out_shape=[jax.ShapeDtypeStruct((HB, rows, KV_LORA), F32),
                   jax.ShapeDtypeStruct((HB, rows, DR), F32),
                   jax.ShapeDtypeStruct((rows, KV_LORA), F32),
                   jax.ShapeDtypeStruct((rows, DR), F32)],
        compiler_params=_cparams(("parallel",)),
        name="post_proj",
    )(h, h, h, cos_t, sin_t, g_qa.reshape(1, -1), g_kva.reshape(1, -1), wqb, wuk)


def _bucket_np(n):
    n = np.maximum(n, 0)
    half = N_BUCKETS // 2
    nf = np.maximum(n, 1).astype(np.float32)
    large = half + (np.log(nf / np.float32(half)) / np.float32(math.log(MAX_DISTANCE / half))
                    * np.float32(N_BUCKETS - half)).astype(np.int32)
    return np.where(n < half, n, np.minimum(large, N_BUCKETS - 1)).astype(np.int32)


def _t5_kernel(tab_ref, bkt_ref, o_ref):
    bkt = bkt_ref[...]
    for h in range(HA + HC):
        acc = jnp.zeros(bkt.shape, F32)
        for b in range(N_BUCKETS):
            acc = jnp.where(bkt == b, tab_ref[b, h], acc)
        o_ref[h] = acc


def t5_expand(table, buckets, tr):
    r, c = buckets.shape
    return pl.pallas_call(
        _t5_kernel,
        grid=(r // tr,),
        in_specs=[pl.BlockSpec(memory_space=pltpu.SMEM),
                  pl.BlockSpec((tr, c), lambda i: (i, 0))],
        out_specs=pl.BlockSpec((HA + HC, tr, c), lambda i: (0, i, 0)),
        out_shape=jax.ShapeDtypeStruct((HA + HC, r, c), F32),
        name="t5_bias",
    )(table, buckets)


def _flash_update(s, pv, m_ref, l_ref, acc_ref, rs=slice(None)):
    m = m_ref[rs, :]
    m_new = jnp.maximum(m, jnp.max(s, axis=-1, keepdims=True))
    alpha = jnp.exp(m - m_new)
    p = jnp.exp(s - m_new)
    l_ref[rs, :] = alpha * l_ref[rs, :] + jnp.sum(p, axis=-1, keepdims=True)
    acc_ref[rs, :] = alpha * acc_ref[rs, :] + pv(p.astype(BF16))
    m_ref[rs, :] = m_new


def _flash_init(m_ref, l_ref, acc_ref):
    m_ref[...] = jnp.full(m_ref.shape, NEG, F32)
    l_ref[...] = jnp.zeros(l_ref.shape, F32)
    acc_ref[...] = jnp.zeros(acc_ref.shape, F32)


def _count(mask, axes):
    c = jnp.where(mask, 1.0, 0.0)
    for ax in axes:
        c = jnp.sum(c, axis=ax, keepdims=True)
    return c


def _select_topk(sc, idx, topk, axes, nbits, radix_bits):
    sc = jnp.where(sc == 0.0, 0.0, sc)
    bits = pltpu.bitcast(sc, I32)
    key = bits ^ ((bits >> 31) & 0x7FFFFFFF)
    kf = float(topk)
    tshape = tuple(1 if a in axes else d for a, d in enumerate(sc.shape))

    def tbody(i, t):
        shift = 32 - radix_bits * (i + 1)
        new = t
        for v in range(1, 2 ** radix_bits):
            cand = t + jnp.left_shift(jnp.int32(v), shift)
            new = jnp.where(_count(key >= cand, axes) >= kf, cand, new)
        return new

    t = lax.fori_loop(0, 32 // radix_bits, tbody, jnp.full(tshape, INT_MIN, I32))
    gt = key > t
    eq = key == t
    need = kf - _count(gt, axes)
    spare = _count(eq, axes) - need
    contested = jnp.max(jnp.where(t > NINF_KEY, spare, 0.0))

    def last_tie():
        def jbody(i, j):
            cand = j + jnp.left_shift(jnp.int32(1), nbits - 1 - i)
            c = jnp.where(eq, jnp.where(idx < cand, 1.0, 0.0), 0.0)
            for ax in axes:
                c = jnp.sum(c, axis=ax, keepdims=True)
            return jnp.where(c < need, cand, j)
        return lax.fori_loop(0, nbits, jbody, jnp.zeros(tshape, I32))

    jmax = lax.cond(contested > 0.0, last_tie, lambda: jnp.full(tshape, INT_MAX, I32))
    take = jnp.where(gt, 1.0, jnp.where(eq, jnp.where(idx <= jmax, 1.0, 0.0), 0.0))
    take = jnp.where(sc > -jnp.inf, take, 0.0)
    return jnp.where(take > 0.5, 0.0, NEG)


def _split_maps(x):
    lo = lax.broadcasted_iota(I32, (1, LANE), 1) < LANE // 2
    return jnp.where(lo, x, 0.0), jnp.where(lo, 0.0, x)


def _diff_lambda(lq1_ref, lk1_ref, lq2_ref, lk2_ref, lam_init):
    a = jnp.sum(lq1_ref[...] * lk1_ref[...], axis=-1, keepdims=True)
    b = jnp.sum(lq2_ref[...] * lk2_ref[...], axis=-1, keepdims=True)
    return jnp.exp(a) - jnp.exp(b) + lam_init


def _diff_queries(x, rows):
    zero = jnp.zeros((rows, LANE), F32)
    parts = []
    for hc in range(HC):
        for piece in _split_maps(x[:, hc * LANE:(hc + 1) * LANE] * (DC_QK ** -0.5)):
            parts.append(jnp.concatenate([piece, zero] if hc < C_GROUP else [zero, piece], axis=1))
    return jnp.concatenate(parts, axis=0).astype(BF16)


def _diff_pv(p, v, half_rows):
    return jnp.concatenate([_dot(p[:half_rows], v[:, :DC_V]), _dot(p[half_rows:], v[:, DC_V:])], axis=0)


def _diff_finish(o, rows, lam, gsub, lam_init, o_ref):
    for hc in range(HC):
        r0 = 2 * hc * rows
        d = o[r0:r0 + rows] - lam * o[r0 + rows:r0 + 2 * rows]
        o_ref[:, hc * DC_V:(hc + 1) * DC_V] = _rms(d, gsub) * (1.0 - lam_init)


def _zone(j, qi, r):
    return jnp.clip(qi - r * j, 0, r + 1)


def _causal_chunks(qi, r, step):
    last = qi // r

    def body(j, carry):
        step(j, False)
        return carry

    lax.fori_loop(0, last, body, 0)
    step(last, True)


def _mla_p_kernel(qlat_ref, qpe_ref, klat_ref, kpe_ref, wuv_ref, o_ref, m_ref, l_ref, acc_ref, *, tq, tk, scale):
    qi = pl.program_id(1)
    ql = (qlat_ref[...] * scale).reshape(HB * tq, KV_LORA).astype(BF16)
    qp = (qpe_ref[...] * scale).reshape(HB * tq, DR).astype(BF16)
    qpos = qi * tq + lax.broadcasted_iota(I32, (1, tq, 1), 1)
    lane = lax.broadcasted_iota(I32, (1, 1, tk), 2)
    _flash_init(m_ref, l_ref, acc_ref)

    def step(j, masked):
        k0 = pl.multiple_of(j * tk, tk)
        kl = klat_ref[pl.ds(k0, tk), :].astype(BF16)
        kp = kpe_ref[pl.ds(k0, tk), :].astype(BF16)
        hg = HB // MLA_SPLIT
        for g in range(MLA_SPLIT):
            rs = slice(g * hg * tq, (g + 1) * hg * tq)
            s = _dot_nt(ql[rs], kl) + _dot_nt(qp[rs], kp)
            if masked:
                s = jnp.where(k0 + lane <= qpos, s.reshape(hg, tq, tk), NEG).reshape(hg * tq, tk)
            _flash_update(s, lambda p: _dot(p, kl), m_ref, l_ref, acc_ref, rs)

    _causal_chunks(qi, tk // tq, step)
    o = acc_ref[...] / l_ref[...]
    for h in range(HB):
        o_ref[:, h * DV:(h + 1) * DV] = _dot(o[h * tq:(h + 1) * tq].astype(BF16), wuv_ref[h])


def mla_prompt(qlat, qpe, blat, bkpe, wuv, nb, seq, tq, tk):
    nq = seq // tq
    rows = HB * tq
    return pl.pallas_call(
        functools.partial(_mla_p_kernel, tq=tq, tk=tk, scale=(DN + DR) ** -0.5),
        grid=(nb, nq),
        in_specs=[pl.BlockSpec((HB, tq, KV_LORA), lambda b, i: (0, b * nq + i, 0)),
                  pl.BlockSpec((HB, tq, DR), lambda b, i: (0, b * nq + i, 0)),
                  pl.BlockSpec((seq, KV_LORA), lambda b, i: (b, 0)),
                  pl.BlockSpec((seq, DR), lambda b, i: (b, 0)),
                  pl.BlockSpec(wuv.shape, lambda b, i: (0, 0, 0))],
        out_specs=pl.BlockSpec((tq, HB * DV), lambda b, i: (b * nq + i, 0)),
        out_shape=jax.ShapeDtypeStruct((nb * seq, HB * DV), F32),
        scratch_shapes=[pltpu.VMEM((rows, 1), F32), pltpu.VMEM((rows, 1), F32),
                        pltpu.VMEM((rows, KV_LORA), F32)],
        compiler_params=_cparams(("parallel", "arbitrary")),
        name="mla_prompt",
    )(qlat, qpe, blat, bkpe, wuv)


def _diff_p_kernel(cq_ref, ck_ref, cv_ref, bias_ref, lq1_ref, lk1_ref, lq2_ref, lk2_ref, gsub_ref,
                   o_ref, m_ref, l_ref, acc_ref, *, tq, tk, lam_init):
    qi = pl.program_id(1)
    r = tk // tq
    q = _diff_queries(cq_ref[...], tq)
    qpos = qi * tq + lax.broadcasted_iota(I32, (1, 1, tq, 1), 2)
    lane = lax.broadcasted_iota(I32, (1, 1, 1, tk), 3)
    _flash_init(m_ref, l_ref, acc_ref)

    def step(j, masked):
        k0 = pl.multiple_of(j * tk, tk)
        k = ck_ref[pl.ds(k0, tk), :].astype(BF16)
        v = cv_ref[pl.ds(k0, tk), :].astype(BF16)
        s = _dot_nt(q, k).reshape(HC, 2, tq, tk) + bias_ref[_zone(j, qi, r)][:, None]
        if masked:
            s = jnp.where(k0 + lane <= qpos, s, NEG)
        _flash_update(s.reshape(2 * HC * tq, tk), lambda p: _diff_pv(p, v, HC * tq), m_ref, l_ref, acc_ref)

    _causal_chunks(qi, r, step)
    lam = _diff_lambda(lq1_ref, lk1_ref, lq2_ref, lk2_ref, lam_init)
    _diff_finish(acc_ref[...] / l_ref[...], tq, lam, gsub_ref[...], lam_init, o_ref)


def diff_prompt(h, bias, lams, gsub, nb, seq, tq, tk, lam_init):
    nq = seq // tq
    rows = 2 * HC * tq
    vec = pl.BlockSpec((1, DC_QK), lambda b, i: (0, 0))
    return pl.pallas_call(
        functools.partial(_diff_p_kernel, tq=tq, tk=tk, lam_init=lam_init),
        grid=(nb, nq),
        in_specs=[pl.BlockSpec((tq, HC * LANE), lambda b, i: (b * nq + i, C_CQ // (HC * LANE))),
                  pl.BlockSpec((seq, HC_KV * LANE), lambda b, i: (b, C_CK // (HC_KV * LANE))),
                  pl.BlockSpec((seq, HC_KV * DC_V), lambda b, i: (b, C_CV // (HC_KV * DC_V))),
                  pl.BlockSpec(bias.shape, lambda b, i: (0, 0, 0, 0)),
                  vec, vec, vec, vec,
                  pl.BlockSpec((1, DC_V), lambda b, i: (0, 0))],
        out_specs=pl.BlockSpec((tq, HC * DC_V), lambda b, i: (b * nq + i, 0)),
        out_shape=jax.ShapeDtypeStruct((nb * seq, HC * DC_V), F32),
        scratch_shapes=[pltpu.VMEM((rows, 1), F32), pltpu.VMEM((rows, 1), F32),
                        pltpu.VMEM((rows, DC_V), F32)],
        compiler_params=_cparams(("parallel", "arbitrary")),
        name="diff_prompt",
    )(h, h, h, bias, *lams, gsub.reshape(1, -1))


def _dsa_p_kernel(aq_ref, iq_ref, iw_ref, ik2_ref, ak_ref, av_ref, bias_ref, o_ref,
                  sct_ref, sb_ref, m_ref, l_ref, acc_ref, *, tq, tk, nc, topk):
    qi = pl.program_id(1)
    r = tk // tq
    iq = iq_ref[...]
    parts = []
    for p in range(H_IDX // 2):
        parts += list(_split_maps(iq[:, p * LANE:(p + 1) * LANE]))
    xq = jnp.concatenate(parts, axis=0).astype(BF16)
    iwt = jnp.transpose(iw_ref[...])[:H_IDX] * (D_IDX ** -0.5 * H_IDX ** -0.5)
    qpos = qi * tq + lax.broadcasted_iota(I32, (1, tq), 1)
    krow = lax.broadcasted_iota(I32, (LANE, 1), 0)
    sct_ref[...] = jnp.full(sct_ref.shape, -jnp.inf, F32)

    def sbody(j, carry):
        k0 = pl.multiple_of(j * LANE, LANE)
        k2 = ik2_ref[pl.ds(k0, LANE), :].astype(BF16)
        rt = jnp.maximum(_dot_nt(k2, xq), 0.0)
        s = rt[:, :tq] * iwt[0:1]
        for h in range(1, H_IDX):
            s = s + rt[:, h * tq:(h + 1) * tq] * iwt[h:h + 1]
        sct_ref[j] = jnp.where(k0 + krow <= qpos, s, -jnp.inf)
        return carry

    lax.fori_loop(0, qi + 1, sbody, 0)

    idx = (lax.broadcasted_iota(I32, (nc, LANE, 1), 0) * LANE
           + lax.broadcasted_iota(I32, (nc, LANE, 1), 1))
    sbt = _select_topk(sct_ref[...], idx, topk, (0, 1), max(1, (nc * LANE - 1).bit_length()), 1)
    for c in range(nc):
        sb_ref[c] = jnp.transpose(sbt[c])

    aq = aq_ref[...] * (DA ** -0.5)
    q = jnp.concatenate([aq[:, h * DA:(h + 1) * DA] for h in range(HA)], axis=0).astype(BF16)
    _flash_init(m_ref, l_ref, acc_ref)

    def abody(j, carry):
        k0 = pl.multiple_of(j * tk, tk)
        k = ak_ref[pl.ds(k0, tk), :].astype(BF16)
        v = av_ref[pl.ds(k0, tk), :].astype(BF16)
        sbc = jnp.concatenate([sb_ref[r * j + i] for i in range(r)], axis=1)
        s = _dot_nt(q, k).reshape(HA, tq, tk) + bias_ref[_zone(j, qi, r)] + sbc[None]
        _flash_update(s.reshape(HA * tq, tk), lambda p: _dot(p, v), m_ref, l_ref, acc_ref)
        return carry

    lax.fori_loop(0, qi // r + 1, abody, 0)
    o = acc_ref[...] / l_ref[...]
    for h in range(HA):
        o_ref[:, h * DA:(h + 1) * DA] = o[h * tq:(h + 1) * tq]


def dsa_prompt(h, bias, nb, seq, tk, topk):
    tq = LANE
    nq = seq // tq
    rows = HA * tq
    return pl.pallas_call(
        functools.partial(_dsa_p_kernel, tq=tq, tk=tk, nc=nq, topk=topk),
        grid=(nb, nq),
        in_specs=[pl.BlockSpec((tq, HA * DA), lambda b, i: (b * nq + i, C_AQ // (HA * DA))),
                  pl.BlockSpec((tq, H_IDX * D_IDX), lambda b, i: (b * nq + i, C_IQ // (H_IDX * D_IDX))),
                  pl.BlockSpec((tq, LANE), lambda b, i: (b * nq + i, C_IW // LANE)),
                  pl.BlockSpec((seq, LANE), lambda b, i: (b, C_IK2 // LANE)),
                  pl.BlockSpec((seq, DA), lambda b, i: (b, C_AK // DA)),
                  pl.BlockSpec((seq, DA), lambda b, i: (b, C_AV // DA)),
                  pl.BlockSpec(bias.shape, lambda b, i: (0, 0, 0, 0))],
        out_specs=pl.BlockSpec((tq, HA * DA), lambda b, i: (b * nq + i, 0)),
        out_shape=jax.ShapeDtypeStruct((nb * seq, HA * DA), F32),
        scratch_shapes=[pltpu.VMEM((nq, LANE, tq), F32), pltpu.VMEM((nq, tq, LANE), F32),
                        pltpu.VMEM((rows, 1), F32), pltpu.VMEM((rows, 1), F32),
                        pltpu.VMEM((rows, DA), F32)],
        compiler_params=_cparams(("parallel", "arbitrary")),
        name="dsa_prompt",
    )(h, h, h, h, h, h, bias)


def _page_dst(buf, slot, p, page_rows):
    if page_rows == 0:
        return buf.at[slot, :, pl.ds(p * PAGE, PAGE)]
    return buf.at[slot, pl.ds(p * page_rows, page_rows)]


def _paged_pipeline(pt_ref, n_pages, layer, pools, bufs, page_rows, sem):
    b = pl.program_id(0)
    slot = b % 2

    def copies(bb, sl):
        cps = []
        for p in range(n_pages):
            page = pt_ref[bb, p]
            for a, (pool, buf, pr) in enumerate(zip(pools, bufs, page_rows)):
                cps.append(pltpu.make_async_copy(pool.at[layer, page], _page_dst(buf, sl, p, pr), sem.at[a, sl]))
        return cps

    @pl.when(b == 0)
    def _():
        for cp in copies(b, slot):
            cp.start()

    @pl.when(b + 1 < pl.num_programs(0))
    def _():
        for cp in copies(b + 1, 1 - slot):
            cp.start()

    for cp in copies(b, slot):
        cp.wait()
    return slot


def _pad_rows(x, rows):
    return jnp.concatenate([x, jnp.zeros((rows - x.shape[0], x.shape[1]), x.dtype)], axis=0)


def _new_key_mask(rows, nq):
    q = lax.broadcasted_iota(I32, (rows, 1), 0) % nq
    j = lax.broadcasted_iota(I32, (1, LANE), 1)
    return j <= q


def _key_chunks(past):
    cw = min(CW, past)
    assert past % cw == 0
    return [slice(c * cw, (c + 1) * cw) for c in range(past // cw)]


def _softmax_pv(s_ref, past, values, pv):
    s_all = s_ref[...]
    m = jnp.max(s_all, axis=-1, keepdims=True)
    acc = None
    l = None
    for kc in _key_chunks(past) + [None]:
        ks = slice(past, past + LANE) if kc is None else kc
        p = jnp.exp(s_ref[:, ks] - m)
        part = pv(p.astype(BF16), values(kc))
        lsum = jnp.sum(p, axis=-1, keepdims=True)
        acc = part if acc is None else acc + part
        l = lsum if l is None else l + lsum
    return acc / l


def _mla_s_kernel(pt_ref, qlat_ref, qpe_ref, nlat_ref, nkpe_ref, wuv_ref, lat_hbm, kpet_hbm, o_ref,
                  latbuf, kpetbuf, sem, s_ref, *, layer, n_pages, nq, scale):
    slot = _paged_pipeline(pt_ref, n_pages, layer, (lat_hbm, kpet_hbm), (latbuf, kpetbuf), (PAGE, 0), sem)
    past = n_pages * PAGE
    rows = HB * nq
    ql = (qlat_ref[...] * scale).reshape(rows, KV_LORA).astype(BF16)
    qp = (qpe_ref[...] * scale).reshape(rows, DR).astype(BF16)
    for ks in _key_chunks(past):
        s_ref[:, ks] = (_dot_nt(ql, latbuf[slot, ks, :].astype(BF16))
                        + _dot(qp, kpetbuf[slot, :, ks].astype(BF16)))
    nl = _pad_rows(nlat_ref[...], LANE).astype(BF16)
    npe = _pad_rows(nkpe_ref[...], LANE).astype(BF16)
    s_ref[:, past:] = jnp.where(_new_key_mask(rows, nq), _dot_nt(ql, nl) + _dot_nt(qp, npe), NEG)

    def values(ks):
        return nl if ks is None else latbuf[slot, ks, :].astype(BF16)

    o = _softmax_pv(s_ref, past, values, _dot)
    for h in range(HB):
        o_ref[:, h * DV:(h + 1) * DV] = _dot(o[h * nq:(h + 1) * nq].astype(BF16), wuv_ref[h])


def mla_sample(page_table, qlat, qpe, blat, bkpe, wuv, cache_lat, cache_kpet, layer, nq):
    nb, n_pages = page_table.shape
    past = n_pages * PAGE
    rows = HB * nq
    grid_spec = pltpu.PrefetchScalarGridSpec(
        num_scalar_prefetch=1,
        grid=(nb,),
        in_specs=[pl.BlockSpec((HB, nq, KV_LORA), lambda b, pt: (0, b, 0)),
                  pl.BlockSpec((HB, nq, DR), lambda b, pt: (0, b, 0)),
                  pl.BlockSpec((nq, KV_LORA), lambda b, pt: (b, 0)),
                  pl.BlockSpec((nq, DR), lambda b, pt: (b, 0)),
                  pl.BlockSpec(wuv.shape, lambda b, pt: (0, 0, 0)),
                  pl.BlockSpec(memory_space=pl.ANY),
                  pl.BlockSpec(memory_space=pl.ANY)],
        out_specs=pl.BlockSpec((nq, HB * DV), lambda b, pt: (b, 0)),
        scratch_shapes=[pltpu.VMEM((2, past, KV_LORA), F32),
                        pltpu.VMEM((2, DR, past), F32),
                        pltpu.SemaphoreType.DMA((2, 2)),
                        pltpu.VMEM((rows, past + LANE), F32)])
    return pl.pallas_call(
        functools.partial(_mla_s_kernel, layer=layer, n_pages=n_pages, nq=nq, scale=(DN + DR) ** -0.5),
        grid_spec=grid_spec,
        out_shape=jax.ShapeDtypeStruct((nb * nq, HB * DV), F32),
        compiler_params=_cparams(("arbitrary",)),
        name="mla_sample",
    )(page_table, qlat, qpe, blat, bkpe, wuv, cache_lat, cache_kpet)


def _diff_s_kernel(pt_ref, cq_ref, nck_ref, ncv_ref, bias_ref, bnew_ref, lq1_ref, lk1_ref, lq2_ref, lk2_ref,
                   gsub_ref, ck_hbm, cv_hbm, o_ref, ckbuf, cvbuf, sem, s_ref, *, layer, n_pages, nq, lam_init):
    slot = _paged_pipeline(pt_ref, n_pages, layer, (ck_hbm, cv_hbm), (ckbuf, cvbuf),
                           (HC_KV * PAGE, HC_KV * PAGE), sem)
    past = n_pages * PAGE
    rows = 2 * HC * nq
    q = _diff_queries(cq_ref[...], nq)

    def packed(buf, ks):
        n = ks.stop - ks.start
        heads = [buf[slot, pl.ds(HC_KV * ks.start + kvh, n, stride=HC_KV), :] for kvh in range(HC_KV)]
        return jnp.concatenate(heads, axis=1).astype(BF16)

    for ks in _key_chunks(past):
        s_ref[:, ks] = _dot_nt(q, packed(ckbuf, ks)) + bias_ref[:, ks]
    nk = _pad_rows(nck_ref[...], LANE).astype(BF16)
    nv = _pad_rows(ncv_ref[...], LANE).astype(BF16)
    s_ref[:, past:] = jnp.where(_new_key_mask(rows, nq), _dot_nt(q, nk) + bnew_ref[...], NEG)

    def values(ks):
        return nv if ks is None else packed(cvbuf, ks)

    o = _softmax_pv(s_ref, past, values, lambda p, v: _diff_pv(p, v, HC * nq))
    lam = _diff_lambda(lq1_ref, lk1_ref, lq2_ref, lk2_ref, lam_init)
    _diff_finish(o, nq, lam, gsub_ref[...], lam_init, o_ref)


def diff_sample(page_table, h, bias, bias_new, lams, gsub, cache_ck, cache_cv, layer, nq, lam_init):
    nb, n_pages = page_table.shape
    past = n_pages * PAGE
    rows = 2 * HC * nq
    vec = pl.BlockSpec((1, DC_QK), lambda b, pt: (0, 0))
    grid_spec = pltpu.PrefetchScalarGridSpec(
        num_scalar_prefetch=1,
        grid=(nb,),
        in_specs=[pl.BlockSpec((nq, HC * LANE), lambda b, pt: (b, C_CQ // (HC * LANE))),
                  pl.BlockSpec((nq, HC_KV * LANE), lambda b, pt: (b, C_CK // (HC_KV * LANE))),
                  pl.BlockSpec((nq, HC_KV * DC_V), lambda b, pt: (b, C_CV // (HC_KV * DC_V))),
                  pl.BlockSpec(bias.shape, lambda b, pt: (0, 0)),
                  pl.BlockSpec(bias_new.shape, lambda b, pt: (0, 0)),
                  vec, vec, vec, vec,
                  pl.BlockSpec((1, DC_V), lambda b, pt: (0, 0)),
                  pl.BlockSpec(memory_space=pl.ANY),
                  pl.BlockSpec(memory_space=pl.ANY)],
        out_specs=pl.BlockSpec((nq, HC * DC_V), lambda b, pt: (b, 0)),
        scratch_shapes=[pltpu.VMEM((2, past * HC_KV, LANE), F32),
                        pltpu.VMEM((2, past * HC_KV, DC_V), F32),
                        pltpu.SemaphoreType.DMA((2, 2)),
                        pltpu.VMEM((rows, past + LANE), F32)])
    return pl.pallas_call(
        functools.partial(_diff_s_kernel, layer=layer, n_pages=n_pages, nq=nq, lam_init=lam_init),
        grid_spec=grid_spec,
        out_shape=jax.ShapeDtypeStruct((nb * nq, HC * DC_V), F32),
        compiler_params=_cparams(("arbitrary",)),
        name="diff_sample",
    )(page_table, h, h, h, bias, bias_new, *lams, gsub.reshape(1, -1), cache_ck, cache_cv)


def _dsa_s_kernel(pt_ref, aq_ref, iq_ref, iw_ref, nik_ref, nak_ref, nav_ref, bias_ref, bnew_ref,
                  kidxt_hbm, ak_hbm, av_hbm, o_ref, kidxtbuf, akbuf, avbuf, sem, sc_ref, s_ref,
                  *, layer, n_pages, nq, topk):
    slot = _paged_pipeline(pt_ref, n_pages, layer, (kidxt_hbm, ak_hbm, av_hbm), (kidxtbuf, akbuf, avbuf),
                           (0, PAGE, PAGE), sem)
    past = n_pages * PAGE
    iq = iq_ref[...]
    iw = iw_ref[...] * (D_IDX ** -0.5 * H_IDX ** -0.5)
    xq = jnp.concatenate([iq[:, h * D_IDX:(h + 1) * D_IDX] for h in range(H_IDX)], axis=0).astype(BF16)
    iwc = jnp.concatenate([iw[:, h:h + 1] for h in range(H_IDX)], axis=0)

    def scores(r):
        return jnp.sum((jnp.maximum(r, 0.0) * iwc).reshape(H_IDX, nq, r.shape[1]), axis=0)

    for ks in _key_chunks(past):
        sc_ref[:, ks] = scores(_dot(xq, kidxtbuf[slot, :, ks].astype(BF16)))
    s_new = scores(_dot_nt(xq, _pad_rows(nik_ref[...][:, :D_IDX], LANE).astype(BF16)))
    sc_ref[:, past:] = jnp.where(_new_key_mask(nq, nq), s_new, -jnp.inf)

    idx = lax.broadcasted_iota(I32, (1, past + LANE), 1)
    sb = _select_topk(sc_ref[...], idx, topk, (1,), (past + LANE - 1).bit_length(), 2)

    aq = aq_ref[...] * (DA ** -0.5)
    q = jnp.concatenate([aq[:, h * DA:(h + 1) * DA] for h in range(HA)], axis=0).astype(BF16)
    rows = HA * nq

    def logits(k, bias, sbc):
        return ((_dot_nt(q, k)).reshape(HA, nq, k.shape[0]) + bias + sbc[None]).reshape(rows, k.shape[0])

    for ks in _key_chunks(past):
        s_ref[:, ks] = logits(akbuf[slot, ks, :].astype(BF16), bias_ref[:, :, ks], sb[:, ks])
    s_ref[:, past:] = logits(_pad_rows(nak_ref[...], LANE).astype(BF16), bnew_ref[...], sb[:, past:])
    nv = _pad_rows(nav_ref[...], LANE).astype(BF16)

    def values(ks):
        return nv if ks is None else avbuf[slot, ks, :].astype(BF16)

    o = _softmax_pv(s_ref, past, values, _dot)
    for h in range(HA):
        o_ref[:, h * DA:(h + 1) * DA] = o[h * nq:(h + 1) * nq]


def dsa_sample(page_table, h, bias, bias_new, cache_kidxt, cache_ak, cache_av, layer, nq, topk):
    nb, n_pages = page_table.shape
    past = n_pages * PAGE
    rows = HA * nq
    grid_spec = pltpu.PrefetchScalarGridSpec(
        num_scalar_prefetch=1,
        grid=(nb,),
        in_specs=[pl.BlockSpec((nq, HA * DA), lambda b, pt: (b, C_AQ // (HA * DA))),
                  pl.BlockSpec((nq, H_IDX * D_IDX), lambda b, pt: (b, C_IQ // (H_IDX * D_IDX))),
                  pl.BlockSpec((nq, LANE), lambda b, pt: (b, C_IW // LANE)),
                  pl.BlockSpec((nq, LANE), lambda b, pt: (b, C_IK2 // LANE)),
                  pl.BlockSpec((nq, DA), lambda b, pt: (b, C_AK // DA)),
                  pl.BlockSpec((nq, DA), lambda b, pt: (b, C_AV // DA)),
                  pl.BlockSpec(bias.shape, lambda b, pt: (0, 0, 0)),
                  pl.BlockSpec(bias_new.shape, lambda b, pt: (0, 0, 0)),
                  pl.BlockSpec(memory_space=pl.ANY),
                  pl.BlockSpec(memory_space=pl.ANY),
                  pl.BlockSpec(memory_space=pl.ANY)],
        out_specs=pl.BlockSpec((nq, HA * DA), lambda b, pt: (b, 0)),
        scratch_shapes=[pltpu.VMEM((2, D_IDX, past), F32),
                        pltpu.VMEM((2, past, DA), F32),
                        pltpu.VMEM((2, past, DA), F32),
                        pltpu.SemaphoreType.DMA((3, 2)),
                        pltpu.VMEM((nq, past + LANE), F32),
                        pltpu.VMEM((rows, past + LANE), F32)])
    return pl.pallas_call(
        functools.partial(_dsa_s_kernel, layer=layer, n_pages=n_pages, nq=nq, topk=topk),
        grid_spec=grid_spec,
        out_shape=jax.ShapeDtypeStruct((nb * nq, HA * DA), F32),
        compiler_params=_cparams(("arbitrary",)),
        name="dsa_sample",
    )(page_table, h, h, h, h, h, h, bias, bias_new, cache_kidxt, cache_ak, cache_av)


def _out_kernel(a_ref, b_ref, c_ref, x_ref, w_ref, g_ref, o_ref):
    na = HA * HEAD_DIM
    nab = na + HB * DV
    y = (_dot(a_ref[...].astype(BF16), w_ref[:na, :])
         + _dot(b_ref[...].astype(BF16), w_ref[na:nab, :])
         + _dot(c_ref[...].astype(BF16), w_ref[nab:, :]))
    o_ref[...] = x_ref[...] + _rms(y, g_ref[...])


def out_project(a, b, c, x, w, g):
    rows = x.shape[0]
    tm = _row_tile(rows, 256)
    row = lambda i: (i, 0)
    return pl.pallas_call(
        _out_kernel,
        grid=(rows // tm,),
        in_specs=[pl.BlockSpec((tm, a.shape[1]), row), pl.BlockSpec((tm, b.shape[1]), row),
                  pl.BlockSpec((tm, c.shape[1]), row), pl.BlockSpec((tm, D_MODEL), row),
                  pl.BlockSpec(w.shape, lambda i: (0, 0)), pl.BlockSpec((1, D_MODEL), lambda i: (0, 0))],
        out_specs=pl.BlockSpec((tm, D_MODEL), row),
        out_shape=jax.ShapeDtypeStruct((rows, D_MODEL), F32),
        compiler_params=_cparams(("parallel",)),
        name="out_proj",
    )(a, b, c, x, w, g.reshape(1, -1))


def _ffn_kernel(x_ref, gpre_ref, wg_ref, wu_ref, wd_ref, gpost_ref, o_ref, xn_ref, acc_ref):
    j = pl.program_id(1)

    @pl.when(j == 0)
    def _():
        xn_ref[...] = _rms(x_ref[...], gpre_ref[...]).astype(BF16)
        acc_ref[...] = jnp.zeros(acc_ref.shape, F32)

    xn = xn_ref[...]
    a = _dot(xn, wg_ref[...])
    u = _dot(xn, wu_ref[...])
    hmid = (a * (1.0 / (1.0 + jnp.exp(-a)))) * u
    acc_ref[...] += _dot(hmid.astype(BF16), wd_ref[...])

    @pl.when(j == pl.num_programs(1) - 1)
    def _():
        o_ref[...] = x_ref[...] + _rms(acc_ref[...], gpost_ref[...])


def ffn(x, g_pre, wg, wu, wd, g_post, tf):
    rows = x.shape[0]
    dff = wg.shape[1]
    tm = _row_tile(rows, 512)
    return pl.pallas_call(
        _ffn_kernel,
        grid=(rows // tm, dff // tf),
        in_specs=[pl.BlockSpec((tm, D_MODEL), lambda i, j: (i, 0)),
                  pl.BlockSpec((1, D_MODEL), lambda i, j: (0, 0)),
                  pl.BlockSpec((D_MODEL, tf), lambda i, j: (0, j)),
                  pl.BlockSpec((D_MODEL, tf), lambda i, j: (0, j)),
                  pl.BlockSpec((tf, D_MODEL), lambda i, j: (j, 0)),
                  pl.BlockSpec((1, D_MODEL), lambda i, j: (0, 0))],
        out_specs=pl.BlockSpec((tm, D_MODEL), lambda i, j: (i, 0)),
        out_shape=jax.ShapeDtypeStruct((rows, D_MODEL), F32),
        scratch_shapes=[pltpu.VMEM((tm, D_MODEL), BF16), pltpu.VMEM((tm, D_MODEL), F32)],
        compiler_params=_cparams(("parallel", "arbitrary")),
        name="ffn",
    )(x, g_pre.reshape(1, -1), wg, wu, wd, g_post.reshape(1, -1))


def _rope_tables(pos, rows):
    half = DR // 2
    inv = jnp.power(ROPE_THETA, -jnp.arange(half, dtype=F32) / half)
    ang = pos.astype(F32)[:, None] * inv[None, :]
    cos, sin = jnp.cos(ang), jnp.sin(ang)
    c = jnp.tile(jnp.concatenate([cos, cos], axis=-1), (rows // pos.shape[0], HB))
    s = jnp.tile(jnp.concatenate([-sin, sin], axis=-1), (rows // pos.shape[0], HB))
    return c, s


def _prompt_buckets(tq, tk):
    i = np.arange(tq)[:, None]
    j = np.arange(tk)[None, :]
    zones = [i - j + z * tq for z in range(tk // tq + 1)] + [np.full((tq, tk), MAX_DISTANCE, np.int64)]
    return _bucket_np(np.concatenate(zones, axis=0))


def _sample_buckets(nq, past):
    qs = np.arange(nq)[:, None]
    j = np.arange(PAGE)[None, :]
    last = (past + qs) - (past - PAGE + j)
    new = np.maximum(qs - j, 0)
    return _bucket_np(np.concatenate([last, new], axis=0))


def kernel(x_prompt, x_sample, cache_a_k, cache_a_v, cache_a_kidx, cache_b_latent, cache_b_krope, cache_c_k, cache_c_v, page_table, rel_bias, norm_pre_mix, w_in, q_a_norm, w_q_b, kv_a_norm, w_uk, w_uv, lambda_q1, lambda_k1, lambda_q2, lambda_k2, c_subln, w_out, norm_post_mix, norm_pre_ffn, w_gate, w_up, w_down, norm_post_ffn):
    nb_p, seq, _ = x_prompt.shape
    nb_s, nq_s, _ = x_sample.shape
    depth = w_in.shape[0]
    n_pages = page_table.shape[1]
    past = n_pages * PAGE
    topk_p = min(TOPK_MAX, seq // 4)
    topk_s = min(TOPK_MAX, (past + nq_s) // 4)
    tq = TQ
    tk = min(TK, seq)
    assert seq % tk == 0 and tk % tq == 0 and nq_s == 8

    perm = _in_perm()
    w_in_r = (jnp.take(w_in, jnp.asarray(np.maximum(perm, 0)), axis=2)
              * jnp.asarray(perm >= 0, F32)).astype(BF16)
    w_qb_r = jnp.take(w_q_b, jnp.asarray(_qb_perm()), axis=2).astype(BF16)
    w_uk_r = jnp.transpose(w_uk, (0, 2, 3, 1)).astype(BF16)
    w_uv_r = jnp.transpose(w_uv, (0, 2, 1, 3)).astype(BF16)
    w_out_b = w_out.astype(BF16)
    w_gate_b = w_gate.astype(BF16)
    w_up_b = w_up.astype(BF16)
    w_down_b = w_down.astype(BF16)
    cache_kidxt = jnp.swapaxes(cache_a_kidx, 2, 3)
    cache_kpet = jnp.swapaxes(cache_b_krope, 2, 3)
    cache_ck2 = cache_c_k.reshape(cache_c_k.shape[:2] + (PAGE * HC_KV, 2 * DC_QK))
    cache_cv2 = cache_c_v.reshape(cache_c_v.shape[:2] + (PAGE * HC_KV, DC_V))

    rows_p = nb_p * seq
    rows_s = nb_s * nq_s
    tm_p = _row_tile(rows_p, 512)
    tm_s = _row_tile(rows_s, 512)
    cos_p, sin_p = _rope_tables(jnp.arange(seq, dtype=I32), max(seq, tm_p))
    cos_s, sin_s = _rope_tables(past + jnp.arange(nq_s, dtype=I32), max(nq_s, tm_s))

    nz = tk // tq + 2
    zones = t5_expand(rel_bias, jnp.asarray(_prompt_buckets(tq, tk)), tq).reshape(HA + HC, nz, tq, tk)
    zones = jnp.transpose(zones, (1, 0, 2, 3))
    bias_a_p, bias_c_p = zones[:, :HA], zones[:, HA:]
    tiles = t5_expand(rel_bias, jnp.asarray(_sample_buckets(nq_s, past)), 2 * nq_s)
    last, new = tiles[:, :nq_s], tiles[:, nq_s:]
    far = jnp.broadcast_to(zones[nz - 1, :, :1, :1], (HA + HC, nq_s, past - PAGE))
    full = jnp.concatenate([far, last], axis=-1)
    bias_a_s, bias_a_new = full[:HA], new[:HA]
    rep = lambda t: jnp.repeat(t[HA:], 2, axis=0).reshape(2 * HC * nq_s, t.shape[-1])
    bias_c_s, bias_c_new = rep(full), rep(new)

    hp = x_prompt.reshape(rows_p, D_MODEL)
    hs = x_sample.reshape(rows_s, D_MODEL)
    new_p = [[] for _ in range(7)]
    new_s = [[] for _ in range(7)]
    for l in range(depth):
        lam_init = 0.8 - 0.6 * math.exp(-0.3 * l)
        lams = tuple(v[l].reshape(1, DC_QK) for v in (lambda_q1, lambda_k1, lambda_q2, lambda_k2))
        outs = []
        for grp, (x, cos_t, sin_t) in enumerate(((hp, cos_p, sin_p), (hs, cos_s, sin_s))):
            h = norm_matmul(x, norm_pre_mix[l], w_in_r[l], IN_COLS_R // 3)
            qlat, qpe, blat, bkpe = post_project(h, cos_t, sin_t, q_a_norm[l], kv_a_norm[l], w_qb_r[l], w_uk_r[l])
            if grp == 0:
                a_o = dsa_prompt(h, bias_a_p, nb_p, seq, tk, topk_p)
                b_o = mla_prompt(qlat, qpe, blat, bkpe, w_uv_r[l], nb_p, seq, tq, tk)
                c_o = diff_prompt(h, bias_c_p, lams, c_subln[l], nb_p, seq, tq, tk, lam_init)
            else:
                a_o = dsa_sample(page_table, h, bias_a_s, bias_a_new, cache_kidxt, cache_a_k, cache_a_v,
                                 l, nq_s, topk_s)
                b_o = mla_sample(page_table, qlat, qpe, blat, bkpe, w_uv_r[l], cache_b_latent, cache_kpet,
                                 l, nq_s)
                c_o = diff_sample(page_table, h, bias_c_s, bias_c_new, lams, c_subln[l], cache_ck2, cache_cv2,
                                  l, nq_s, lam_init)
            x1 = out_project(a_o, b_o, c_o, x, w_out_b[l], norm_post_mix[l])
            outs.append(ffn(x1, norm_pre_ffn[l], w_gate_b[l], w_up_b[l], w_down_b[l], norm_post_ffn[l], 512))
            rows = (h[:, C_AK:C_AK + DA], h[:, C_AV:C_AV + DA], h[:, C_IK2:C_IK2 + D_IDX], blat, bkpe,
                    h[:, C_CK:C_CK + HC_KV * LANE], h[:, C_CV:C_CV + HC_KV * DC_V])
            for i in range(7):
                (new_p if grp == 0 else new_s)[i].append(rows[i])
        hp, hs = outs

    def stack(parts, lead, tail):
        return jnp.stack(parts).reshape((depth,) + lead + tail)

    tails = ((DA,), (DA,), (D_IDX,), (KV_LORA,), (DR,), (HC_KV, 2 * DC_QK), (HC_KV, DC_V))
    outs_p = tuple(stack(new_p[i], (nb_p, seq), tails[i]) for i in range(7))
    outs_s = tuple(stack(new_s[i], (nb_s, nq_s), tails[i]) for i in range(7))
    return (hp.reshape(x_prompt.shape), hs.reshape(x_sample.shape)) + outs_p + outs_s
```

```python
import functools
import math

import numpy as np
import jax
import jax.numpy as jnp
from jax import lax
from jax.experimental import pallas as pl
from jax.experimental.pallas import tpu as pltpu

F32 = jnp.float32
BF16 = jnp.bfloat16
I32 = jnp.int32

D_MODEL = 2048
HEAD_DIM = 128
HA = 4
HC = 4
HB = 8
DA = 128
H_IDX = 8
D_IDX = 64
TOPK_MAX = 256
Q_LORA = 512
KV_LORA = 256
DN = 128
DR = 64
DV = 128
HC_KV = 2
C_GROUP = 2
DC_QK = 64
DC_V = 128
N_BUCKETS = 32
MAX_DISTANCE = 128
ROPE_THETA = 10000.0
EPS = 1e-6
PAGE = 128

LANE = 128
NEG = -1e30
INT_MIN = -2 ** 31
INT_MAX = 2 ** 31 - 1
NINF_KEY = -2139095041
VMEM_LIMIT = 52 * 1024 * 1024
TQ = 128
TK = 512
CW = 2048
MLA_SPLIT = 2

C_AQ, C_IQ, C_BQ, C_CQ = 0, 512, 1024, 1536
C_LAT, C_CK, C_CV = 2048, 2304, 2560
C_AK, C_AV, C_IK2, C_KPE2, C_IW = 2816, 2944, 3072, 3200, 3328
IN_COLS_R = 3456
_O_AQ, _O_AK, _O_AV, _O_IQ, _O_IK, _O_IW, _O_BQ, _O_BKV, _O_CQ, _O_CK, _O_CV = (
    0, 512, 640, 768, 1280, 1344, 1352, 1864, 2184, 2696, 2952)


def _in_perm():
    perm = np.full((IN_COLS_R,), -1, np.int64)
    def put(dst, src, n):
        perm[dst:dst + n] = np.arange(src, src + n)
    put(C_AQ, _O_AQ, 512)
    put(C_IQ, _O_IQ, 512)
    put(C_BQ, _O_BQ, 512)
    put(C_CQ, _O_CQ, 512)
    put(C_LAT, _O_BKV, 256)
    put(C_CK, _O_CK, 256)
    put(C_CV, _O_CV, 256)
    put(C_AK, _O_AK, 128)
    put(C_AV, _O_AV, 128)
    put(C_IK2, _O_IK, 64)
    put(C_IK2 + 64, _O_IK, 64)
    kpe = _O_BKV + KV_LORA
    put(C_KPE2, kpe, 64)
    put(C_KPE2 + 64, kpe + 32, 32)
    put(C_KPE2 + 96, kpe, 32)
    put(C_IW, _O_IW, 8)
    return perm


def _qb_perm():
    nope, pe, pes = [], [], []
    for h in range(HB):
        base = h * (DN + DR)
        nope += list(range(base, base + DN))
        pe += list(range(base + DN, base + DN + DR))
        pes += list(range(base + DN + 32, base + DN + 64)) + list(range(base + DN, base + DN + 32))
    return np.array(nope + pe + pes, np.int64)


def _dot(a, b):
    return jnp.dot(a, b, preferred_element_type=F32)


def _dot_nt(a, b):
    return lax.dot_general(a, b, (((1,), (1,)), ((), ())), preferred_element_type=F32)


def _rms(x, g):
    return (x * lax.rsqrt(jnp.mean(x * x, axis=-1, keepdims=True) + EPS)) * g


def _cparams(sem):
    return pltpu.CompilerParams(dimension_semantics=sem, vmem_limit_bytes=VMEM_LIMIT)


def _row_tile(rows, want):
    t = min(rows, want)
    assert rows % t == 0, (rows, t)
    return t


def _norm_matmul_kernel(x_ref, g_ref, w_ref, o_ref, xn_ref):
    @pl.when(pl.program_id(1) == 0)
    def _():
        xn_ref[...] = _rms(x_ref[...], g_ref[...]).astype(BF16)
    o_ref[...] = _dot(xn_ref[...], w_ref[...])


def norm_matmul(x, g, w, tn):
    m, k = x.shape
    n = w.shape[1]
    tm = _row_tile(m, 512)
    return pl.pallas_call(
        _norm_matmul_kernel,
        grid=(m // tm, n // tn),
        in_specs=[pl.BlockSpec((tm, k), lambda i, j: (i, 0)),
                  pl.BlockSpec((1, k), lambda i, j: (0, 0)),
                  pl.BlockSpec((k, tn), lambda i, j: (0, j))],
        out_specs=pl.BlockSpec((tm, tn), lambda i, j: (i, j)),
        out_shape=jax.ShapeDtypeStruct((m, n), F32),
        scratch_shapes=[pltpu.VMEM((tm, k), BF16)],
        compiler_params=_cparams(("parallel", "arbitrary")),
        name="in_proj",
    )(x, g.reshape(1, k), w)


def _post_kernel(bq_ref, lat_ref, kpe2_ref, cos_ref, sin_ref, gq_ref, gkv_ref, wqb_ref, wuk_ref, *rest,
                 with_t):
    if with_t:
        av_ref, cv_ref, qlat_ref, qpe_ref, blat_ref, bkpe_ref, latt_ref, avt_ref, cvt_ref = rest
        blat = _rms(lat_ref[...], gkv_ref[...])
        latt_ref[0] = jnp.transpose(blat).astype(BF16)
        avt_ref[0] = jnp.transpose(av_ref[...]).astype(BF16)
        cvt_ref[0] = jnp.transpose(cv_ref[...]).astype(BF16)
    else:
        qlat_ref, qpe_ref, blat_ref, bkpe_ref = rest
    qn = _rms(bq_ref[...], gq_ref[...]).astype(BF16)
    qb = _dot(qn, wqb_ref[...])
    cos = cos_ref[...]
    sin = sin_ref[...]
    for h in range(HB):
        qlat_ref[h] = _dot(qb[:, h * DN:(h + 1) * DN].astype(BF16), wuk_ref[h])
    nq = HB * DN
    pe = qb[:, nq:nq + HB * DR] * cos + qb[:, nq + HB * DR:] * sin
    for h in range(HB):
        qpe_ref[h] = pe[:, h * DR:(h + 1) * DR]
    blat_ref[...] = _rms(lat_ref[...], gkv_ref[...])
    k2 = kpe2_ref[...]
    bkpe_ref[...] = k2[:, :DR] * cos[:, :DR] + k2[:, DR:] * sin[:, :DR]


def post_project(h, cos_t, sin_t, g_qa, g_kva, wqb, wuk, with_t):
    rows = h.shape[0]
    tm = _row_tile(rows, 512)
    nt = cos_t.shape[0] // tm
    in_specs = [pl.BlockSpec((tm, Q_LORA), lambda i: (i, C_BQ // Q_LORA)),
                pl.BlockSpec((tm, KV_LORA), lambda i: (i, C_LAT // KV_LORA)),
                pl.BlockSpec((tm, LANE), lambda i: (i, C_KPE2 // LANE)),
                pl.BlockSpec((tm, HB * DR), lambda i: (i % nt, 0)),
                pl.BlockSpec((tm, HB * DR), lambda i: (i % nt, 0)),
                pl.BlockSpec((1, Q_LORA), lambda i: (0, 0)),
                pl.BlockSpec((1, KV_LORA), lambda i: (0, 0)),
                pl.BlockSpec(wqb.shape, lambda i: (0, 0)),
                pl.BlockSpec(wuk.shape, lambda i: (0, 0, 0))]
    out_specs = [pl.BlockSpec((HB, tm, KV_LORA), lambda i: (0, i, 0)),
                 pl.BlockSpec((HB, tm, DR), lambda i: (0, i, 0)),
                 pl.BlockSpec((tm, KV_LORA), lambda i: (i, 0)),
                 pl.BlockSpec((tm, DR), lambda i: (i, 0))]
    out_shape = [jax.ShapeDtypeStruct((HB, rows, KV_LORA), F32),
                 jax.ShapeDtypeStruct((HB, rows, DR), F32),
                 jax.ShapeDtypeStruct((rows, KV_LORA), F32),
                 jax.ShapeDtypeStruct((rows, DR), F32)]
    args = [h, h, h, cos_t, sin_t, g_qa.reshape(1, -1), g_kva.reshape(1, -1), wqb, wuk]
    if with_t:
        in_specs += [pl.BlockSpec((tm, DA), lambda i: (i, C_AV // DA)),
                     pl.BlockSpec((tm, HC_KV * DC_V), lambda i: (i, C_CV // (HC_KV * DC_V)))]
        args += [h, h]
        for d in (KV_LORA, DA, HC_KV * DC_V):
            out_specs.append(pl.BlockSpec((1, d, tm), lambda i: (i, 0, 0)))
            out_shape.append(jax.ShapeDtypeStruct((rows // tm, d, tm), BF16))
    return pl.pallas_call(
        functools.partial(_post_kernel, with_t=with_t),
        grid=(rows // tm,),
        in_specs=in_specs,
        out_specs=out_specs,
        out_shape=out_shape,
        compiler_params=_cparams(("parallel",)),
        name="post_proj",
    )(*args)


def _bucket_np(n):
    n = np.maximum(n, 0)
    half = N_BUCKETS // 2
    nf = np.maximum(n, 1).astype(np.float32)
    large = half + (np.log(nf / np.float32(half)) / np.float32(math.log(MAX_DISTANCE / half))
                    * np.float32(N_BUCKETS - half)).astype(np.int32)
    return np.where(n < half, n, np.minimum(large, N_BUCKETS - 1)).astype(np.int32)


def _t5_kernel(tab_ref, bkt_ref, o_ref):
    bkt = bkt_ref[...]
    for h in range(HA + HC):
        acc = jnp.zeros(bkt.shape, F32)
        for b in range(N_BUCKETS):
            acc = jnp.where(bkt == b, tab_ref[b, h], acc)
        o_ref[h] = acc


def t5_expand(table, buckets, tr):
    r, c = buckets.shape
    return pl.pallas_call(
        _t5_kernel,
        grid=(r // tr,),
        in_specs=[pl.BlockSpec(memory_space=pltpu.SMEM),
                  pl.BlockSpec((tr, c), lambda i: (i, 0))],
        out_specs=pl.BlockSpec((HA + HC, tr, c), lambda i: (0, i, 0)),
        out_shape=jax.ShapeDtypeStruct((HA + HC, r, c), F32),
        name="t5_bias",
    )(table, buckets)


def _flash_update(s, pv, m_ref, l_ref, acc_ref, rs=slice(None)):
    m = m_ref[rs, :]
    m_new = jnp.maximum(m, jnp.max(s, axis=-1, keepdims=True))
    alpha = jnp.exp(m - m_new)
    p = jnp.exp(s - m_new)
    l_ref[rs, :] = alpha * l_ref[rs, :] + jnp.sum(p, axis=-1, keepdims=True)
    acc_ref[rs, :] = alpha * acc_ref[rs, :] + pv(p.astype(BF16))
    m_ref[rs, :] = m_new


def _flash_update_t(st, pv, m_ref, l_ref, acc_ref):
    m = m_ref[...]
    m_new = jnp.maximum(m, jnp.max(st, axis=0, keepdims=True))
    alpha = jnp.exp(m - m_new)
    p = jnp.exp(st - m_new)
    l_ref[...] = alpha * l_ref[...] + jnp.sum(p, axis=0, keepdims=True)
    acc_ref[...] = alpha * acc_ref[...] + pv(p.astype(BF16))
    m_ref[...] = m_new


def _flash_init(m_ref, l_ref, acc_ref):
    m_ref[...] = jnp.full(m_ref.shape, NEG, F32)
    l_ref[...] = jnp.zeros(l_ref.shape, F32)
    acc_ref[...] = jnp.zeros(acc_ref.shape, F32)


def _count(mask, axes):
    c = jnp.where(mask, 1.0, 0.0)
    for ax in axes:
        c = jnp.sum(c, axis=ax, keepdims=True)
    return c


def _select_topk(sc, idx, topk, axes, nbits, radix_bits):
    sc = jnp.where(sc == 0.0, 0.0, sc)
    bits = pltpu.bitcast(sc, I32)
    key = bits ^ ((bits >> 31) & 0x7FFFFFFF)
    kf = float(topk)
    tshape = tuple(1 if a in axes else d for a, d in enumerate(sc.shape))

    def tbody(i, t):
        shift = 32 - radix_bits * (i + 1)
        new = t
        for v in range(1, 2 ** radix_bits):
            cand = t + jnp.left_shift(jnp.int32(v), shift)
            new = jnp.where(_count(key >= cand, axes) >= kf, cand, new)
        return new

    t = lax.fori_loop(0, 32 // radix_bits, tbody, jnp.full(tshape, INT_MIN, I32))
    gt = key > t
    eq = key == t
    need = kf - _count(gt, axes)
    spare = _count(eq, axes) - need
    contested = jnp.max(jnp.where(t > NINF_KEY, spare, 0.0))

    def last_tie():
        def jbody(i, j):
            cand = j + jnp.left_shift(jnp.int32(1), nbits - 1 - i)
            c = jnp.where(eq, jnp.where(idx < cand, 1.0, 0.0), 0.0)
            for ax in axes:
                c = jnp.sum(c, axis=ax, keepdims=True)
            return jnp.where(c < need, cand, j)
        return lax.fori_loop(0, nbits, jbody, jnp.zeros(tshape, I32))

    jmax = lax.cond(contested > 0.0, last_tie, lambda: jnp.full(tshape, INT_MAX, I32))
    take = jnp.where(gt, 1.0, jnp.where(eq, jnp.where(idx <= jmax, 1.0, 0.0), 0.0))
    take = jnp.where(sc > -jnp.inf, take, 0.0)
    return jnp.where(take > 0.5, 0.0, NEG)


def _split_maps(x):
    lo = lax.broadcasted_iota(I32, (1, LANE), 1) < LANE // 2
    return jnp.where(lo, x, 0.0), jnp.where(lo, 0.0, x)


def _diff_lambda(lq1_ref, lk1_ref, lq2_ref, lk2_ref, lam_init):
    a = jnp.sum(lq1_ref[...] * lk1_ref[...], axis=-1, keepdims=True)
    b = jnp.sum(lq2_ref[...] * lk2_ref[...], axis=-1, keepdims=True)
    return jnp.exp(a) - jnp.exp(b) + lam_init


def _diff_queries(x, rows):
    zero = jnp.zeros((rows, LANE), F32)
    parts = []
    for hc in range(HC):
        for piece in _split_maps(x[:, hc * LANE:(hc + 1) * LANE] * (DC_QK ** -0.5)):
            parts.append(jnp.concatenate([piece, zero] if hc < C_GROUP else [zero, piece], axis=1))
    return jnp.concatenate(parts, axis=0).astype(BF16)


def _diff_pv(p, v, half_rows):
    return jnp.concatenate([_dot(p[:half_rows], v[:, :DC_V]), _dot(p[half_rows:], v[:, DC_V:])], axis=0)


def _diff_finish(o, rows, lam, gsub, lam_init, o_ref):
    for hc in range(HC):
        r0 = 2 * hc * rows
        d = o[r0:r0 + rows] - lam * o[r0 + rows:r0 + 2 * rows]
        o_ref[:, hc * DC_V:(hc + 1) * DC_V] = _rms(d, gsub) * (1.0 - lam_init)


def _zone(j, qi, r):
    return jnp.clip(qi - r * j, 0, r + 1)


def _causal_chunks(qi, r, step):
    last = qi // r

    def body(j, carry):
        step(j, False)
        return carry

    lax.fori_loop(0, last, body, 0)
    step(last, True)


def _key_major_masks(qi, tq, tk, groups):
    qpos = qi * tq + lax.broadcasted_iota(I32, (1, groups * tq), 1) % tq
    krow = lax.broadcasted_iota(I32, (tk, 1), 0)
    return qpos, krow


def _mla_p_kernel(qlat_ref, qpe_ref, klat_ref, kpe_ref, vt_ref, wuvt_ref, o_ref, m_ref, l_ref, acc_ref,
                  *, tq, tk, scale):
    qi = pl.program_id(1)
    ql = (qlat_ref[...] * scale).reshape(HB * tq, KV_LORA).astype(BF16)
    qp = (qpe_ref[...] * scale).reshape(HB * tq, DR).astype(BF16)
    qpos, krow = _key_major_masks(qi, tq, tk, HB)
    _flash_init(m_ref, l_ref, acc_ref)

    def step(j, masked):
        k0 = pl.multiple_of(j * tk, tk)
        kl = klat_ref[pl.ds(k0, tk), :].astype(BF16)
        kp = kpe_ref[pl.ds(k0, tk), :].astype(BF16)
        st = _dot_nt(kl, ql) + _dot_nt(kp, qp)
        if masked:
            st = jnp.where(k0 + krow <= qpos, st, NEG)
        _flash_update_t(st, lambda p: _dot(vt_ref[j], p), m_ref, l_ref, acc_ref)

    _causal_chunks(qi, tk // tq, step)
    ot = acc_ref[...] / l_ref[...]
    bot = [_dot(wuvt_ref[h], ot[:, h * tq:(h + 1) * tq].astype(BF16)) for h in range(HB)]
    o_ref[...] = jnp.transpose(jnp.concatenate(bot, axis=0))


def mla_prompt(qlat, qpe, blat, bkpe, latt, wuvt, nb, seq, tq, tk):
    nq = seq // tq
    rows = HB * tq
    assert latt.shape[2] == tk
    return pl.pallas_call(
        functools.partial(_mla_p_kernel, tq=tq, tk=tk, scale=(DN + DR) ** -0.5),
        grid=(nb, nq),
        in_specs=[pl.BlockSpec((HB, tq, KV_LORA), lambda b, i: (0, b * nq + i, 0)),
                  pl.BlockSpec((HB, tq, DR), lambda b, i: (0, b * nq + i, 0)),
                  pl.BlockSpec((seq, KV_LORA), lambda b, i: (b, 0)),
                  pl.BlockSpec((seq, DR), lambda b, i: (b, 0)),
                  pl.BlockSpec((seq // tk, KV_LORA, tk), lambda b, i: (b, 0, 0)),
                  pl.BlockSpec(wuvt.shape, lambda b, i: (0, 0, 0))],
        out_specs=pl.BlockSpec((tq, HB * DV), lambda b, i: (b * nq + i, 0)),
        out_shape=jax.ShapeDtypeStruct((nb * seq, HB * DV), F32),
        scratch_shapes=[pltpu.VMEM((1, rows), F32), pltpu.VMEM((1, rows), F32),
                        pltpu.VMEM((KV_LORA, rows), F32)],
        compiler_params=_cparams(("parallel", "arbitrary")),
        name="mla_prompt",
    )(qlat, qpe, blat, bkpe, latt, wuvt)


def _diff_p_kernel(cq_ref, ck_ref, cvt_ref, bias_ref, lq1_ref, lk1_ref, lq2_ref, lk2_ref, gsub_ref,
                   o_ref, m_ref, l_ref, acc_ref, *, tq, tk, lam_init):
    qi = pl.program_id(1)
    r = tk // tq
    q = _diff_queries(cq_ref[...], tq)
    qpos, krow = _key_major_masks(qi, tq, tk, 2 * HC)
    half = HC * tq
    _flash_init(m_ref, l_ref, acc_ref)

    def step(j, masked):
        k0 = pl.multiple_of(j * tk, tk)
        k = ck_ref[pl.ds(k0, tk), :].astype(BF16)
        st = _dot_nt(k, q)
        z = _zone(j, qi, r)
        st = jnp.concatenate([st[:, c * tq:(c + 1) * tq] + bias_ref[z, c // 2] for c in range(2 * HC)], axis=1)
        if masked:
            st = jnp.where(k0 + krow <= qpos, st, NEG)

        def pv(p):
            vt = cvt_ref[j]
            return jnp.concatenate([_dot(vt[:DC_V], p[:, :half]), _dot(vt[DC_V:], p[:, half:])], axis=1)

        _flash_update_t(st, pv, m_ref, l_ref, acc_ref)

    _causal_chunks(qi, r, step)
    lam = _diff_lambda(lq1_ref, lk1_ref, lq2_ref, lk2_ref, lam_init)
    ot = acc_ref[...] / l_ref[...]
    for hc in range(HC):
        c0 = 2 * hc * tq
        d = ot[:, c0:c0 + tq] - lam * ot[:, c0 + tq:c0 + 2 * tq]
        y = (d * lax.rsqrt(jnp.mean(d * d, axis=0, keepdims=True) + EPS)) * gsub_ref[...]
        o_ref[:, hc * DC_V:(hc + 1) * DC_V] = jnp.transpose(y * (1.0 - lam_init))


def diff_prompt(h, cvt, bias, lams, gsub, nb, seq, tq, tk, lam_init):
    nq = seq // tq
    rows = 2 * HC * tq
    assert cvt.shape[2] == tk
    vec = pl.BlockSpec((1, DC_QK), lambda b, i: (0, 0))
    return pl.pallas_call(
        functools.partial(_diff_p_kernel, tq=tq, tk=tk, lam_init=lam_init),
        grid=(nb, nq),
        in_specs=[pl.BlockSpec((tq, HC * LANE), lambda b, i: (b * nq + i, C_CQ // (HC * LANE))),
                  pl.BlockSpec((seq, HC_KV * LANE), lambda b, i: (b, C_CK // (HC_KV * LANE))),
                  pl.BlockSpec((seq // tk, HC_KV * DC_V, tk), lambda b, i: (b, 0, 0)),
                  pl.BlockSpec(bias.shape, lambda b, i: (0, 0, 0, 0)),
                  vec, vec, vec, vec,
                  pl.BlockSpec((DC_V, 1), lambda b, i: (0, 0))],
        out_specs=pl.BlockSpec((tq, HC * DC_V), lambda b, i: (b * nq + i, 0)),
        out_shape=jax.ShapeDtypeStruct((nb * seq, HC * DC_V), F32),
        scratch_shapes=[pltpu.VMEM((1, rows), F32), pltpu.VMEM((1, rows), F32),
                        pltpu.VMEM((DC_V, rows), F32)],
        compiler_params=_cparams(("parallel", "arbitrary")),
        name="diff_prompt",
    )(h, h, cvt, bias, *lams, gsub.reshape(-1, 1))


def _dsa_p_kernel(aq_ref, iq_ref, iw_ref, ik2_ref, ak_ref, avt_ref, bias_ref, o_ref,
                  sct_ref, sbt_ref, m_ref, l_ref, acc_ref, *, tq, tk, nc, topk):
    qi = pl.program_id(1)
    r = tk // tq
    iq = iq_ref[...]
    parts = []
    for p in range(H_IDX // 2):
        parts += list(_split_maps(iq[:, p * LANE:(p + 1) * LANE]))
    xq = jnp.concatenate(parts, axis=0).astype(BF16)
    iwt = jnp.transpose(iw_ref[...])[:H_IDX] * (D_IDX ** -0.5 * H_IDX ** -0.5)
    qpos = qi * tq + lax.broadcasted_iota(I32, (1, tq), 1)
    krow = lax.broadcasted_iota(I32, (LANE, 1), 0)
    sct_ref[...] = jnp.full(sct_ref.shape, -jnp.inf, F32)

    def sbody(j, carry):
        for i in range(r):
            c = j * r + i
            k0 = pl.multiple_of(c * LANE, LANE)
            k2 = ik2_ref[pl.ds(k0, LANE), :].astype(BF16)
            rt = jnp.maximum(_dot_nt(k2, xq), 0.0)
            s = rt[:, :tq] * iwt[0:1]
            for h in range(1, H_IDX):
                s = s + rt[:, h * tq:(h + 1) * tq] * iwt[h:h + 1]
            sct_ref[c] = jnp.where(k0 + krow <= qpos, s, -jnp.inf)
        return carry

    lax.fori_loop(0, qi // r + 1, sbody, 0)

    def select(n):
        def run():
            idx = (lax.broadcasted_iota(I32, (n, LANE, 1), 0) * LANE
                   + lax.broadcasted_iota(I32, (n, LANE, 1), 1))
            sbt_ref[:n] = _select_topk(sct_ref[:n], idx, topk, (0, 1), (n * LANE - 1).bit_length(), 1)
            if n < nc:
                sbt_ref[n:] = jnp.full((nc - n, LANE, tq), NEG, F32)
        return run

    lax.switch(qi // r, [select((g + 1) * r) for g in range(nc // r)])

    aq = aq_ref[...] * (DA ** -0.5)
    q = jnp.concatenate([aq[:, h * DA:(h + 1) * DA] for h in range(HA)], axis=0).astype(BF16)
    _flash_init(m_ref, l_ref, acc_ref)

    def abody(j, carry):
        k0 = pl.multiple_of(j * tk, tk)
        k = ak_ref[pl.ds(k0, tk), :].astype(BF16)
        st = _dot_nt(k, q)
        z = _zone(j, qi, r)
        sbc = jnp.concatenate([sbt_ref[r * j + i] for i in range(r)], axis=0)
        st = jnp.concatenate([st[:, h * tq:(h + 1) * tq] + bias_ref[z, h] + sbc for h in range(HA)], axis=1)
        _flash_update_t(st, lambda p: _dot(avt_ref[j], p), m_ref, l_ref, acc_ref)
        return carry

    lax.fori_loop(0, qi // r + 1, abody, 0)
    ot = acc_ref[...] / l_ref[...]
    for h in range(HA):
        o_ref[:, h * DA:(h + 1) * DA] = jnp.transpose(ot[:, h * tq:(h + 1) * tq])


def dsa_prompt(h, avt, bias, nb, seq, tk, topk):
    tq = LANE
    nq = seq // tq
    rows = HA * tq
    assert avt.shape[2] == tk
    return pl.pallas_call(
        functools.partial(_dsa_p_kernel, tq=tq, tk=tk, nc=nq, topk=topk),
        grid=(nb, nq),
        in_specs=[pl.BlockSpec((tq, HA * DA), lambda b, i: (b * nq + i, C_AQ // (HA * DA))),
                  pl.BlockSpec((tq, H_IDX * D_IDX), lambda b, i: (b * nq + i, C_IQ // (H_IDX * D_IDX))),
                  pl.BlockSpec((tq, LANE), lambda b, i: (b * nq + i, C_IW // LANE)),
                  pl.BlockSpec((seq, LANE), lambda b, i: (b, C_IK2 // LANE)),
                  pl.BlockSpec((seq, DA), lambda b, i: (b, C_AK // DA)),
                  pl.BlockSpec((seq // tk, DA, tk), lambda b, i: (b, 0, 0)),
                  pl.BlockSpec(bias.shape, lambda b, i: (0, 0, 0, 0))],
        out_specs=pl.BlockSpec((tq, HA * DA), lambda b, i: (b * nq + i, 0)),
        out_shape=jax.ShapeDtypeStruct((nb * seq, HA * DA), F32),
        scratch_shapes=[pltpu.VMEM((nq, LANE, tq), F32), pltpu.VMEM((nq, LANE, tq), F32),
                        pltpu.VMEM((1, rows), F32), pltpu.VMEM((1, rows), F32),
                        pltpu.VMEM((DA, rows), F32)],
        compiler_params=_cparams(("parallel", "arbitrary")),
        name="dsa_prompt",
    )(h, h, h, h, h, avt, bias)


def _page_dst(buf, lead, p, page_rows):
    if page_rows == 0:
        return buf.at[lead + (slice(None), pl.ds(p * PAGE, PAGE))]
    return buf.at[lead + (pl.ds(p * page_rows, page_rows),)]


def _paged_pipeline(pt_ref, n_pages, layer, pools, bufs, page_rows, sem, rows_per_step=1):
    b = pl.program_id(0)
    slot = b % 2

    def copies(step, sl):
        cps = []
        for rr in range(rows_per_step):
            lead = (sl,) if rows_per_step == 1 else (sl, rr)
            for p in range(n_pages):
                page = pt_ref[step * rows_per_step + rr, p]
                for a, (pool, buf, pr) in enumerate(zip(pools, bufs, page_rows)):
                    cps.append(pltpu.make_async_copy(pool.at[layer, page], _page_dst(buf, lead, p, pr),
                                                     sem.at[a, sl]))
        return cps

    @pl.when(b == 0)
    def _():
        for cp in copies(b, slot):
            cp.start()

    @pl.when(b + 1 < pl.num_programs(0))
    def _():
        for cp in copies(b + 1, 1 - slot):
            cp.start()

    for cp in copies(b, slot):
        cp.wait()
    return slot


def _pad_rows(x, rows):
    return jnp.concatenate([x, jnp.zeros((rows - x.shape[0], x.shape[1]), x.dtype)], axis=0)


def _new_key_mask(rows, nq):
    q = lax.broadcasted_iota(I32, (rows, 1), 0) % nq
    j = lax.broadcasted_iota(I32, (1, LANE), 1)
    return j <= q


def _key_chunks(past):
    cw = min(CW, past)
    assert past % cw == 0
    return [slice(c * cw, (c + 1) * cw) for c in range(past // cw)]


def _softmax_pv(s_ref, past, values, pv):
    s_all = s_ref[...]
    m = jnp.max(s_all, axis=-1, keepdims=True)
    acc = None
    l = None
    for kc in _key_chunks(past) + [None]:
        ks = slice(past, past + LANE) if kc is None else kc
        p = jnp.exp(s_ref[:, ks] - m)
        part = pv(p.astype(BF16), values(kc))
        lsum = jnp.sum(p, axis=-1, keepdims=True)
        acc = part if acc is None else acc + part
        l = lsum if l is None else l + lsum
    return acc / l


def _mla_s_kernel(pt_ref, qlat_ref, qpe_ref, nlat_ref, nkpe_ref, wuv_ref, lat_hbm, kpet_hbm, o_ref,
                  latbuf, kpetbuf, sem, s_ref, *, layer, n_pages, nq, scale):
    slot = _paged_pipeline(pt_ref, n_pages, layer, (lat_hbm, kpet_hbm), (latbuf, kpetbuf), (PAGE, 0), sem)
    past = n_pages * PAGE
    rows = HB * nq
    ql = (qlat_ref[...] * scale).reshape(rows, KV_LORA).astype(BF16)
    qp = (qpe_ref[...] * scale).reshape(rows, DR).astype(BF16)
    for ks in _key_chunks(past):
        s_ref[:, ks] = (_dot_nt(ql, latbuf[slot, ks, :].astype(BF16))
                        + _dot(qp, kpetbuf[slot, :, ks].astype(BF16)))
    nl = _pad_rows(nlat_ref[...], LANE).astype(BF16)
    npe = _pad_rows(nkpe_ref[...], LANE).astype(BF16)
    s_ref[:, past:] = jnp.where(_new_key_mask(rows, nq), _dot_nt(ql, nl) + _dot_nt(qp, npe), NEG)

    def values(ks):
        return nl if ks is None else latbuf[slot, ks, :].astype(BF16)

    o = _softmax_pv(s_ref, past, values, _dot)
    for h in range(HB):
        o_ref[:, h * DV:(h + 1) * DV] = _dot(o[h * nq:(h + 1) * nq].astype(BF16), wuv_ref[h])


def mla_sample(page_table, qlat, qpe, blat, bkpe, wuv, cache_lat, cache_kpet, layer, nq):
    nb, n_pages = page_table.shape
    past = n_pages * PAGE
    rows = HB * nq
    grid_spec = pltpu.PrefetchScalarGridSpec(
        num_scalar_prefetch=1,
        grid=(nb,),
        in_specs=[pl.BlockSpec((HB, nq, KV_LORA), lambda b, pt: (0, b, 0)),
                  pl.BlockSpec((HB, nq, DR), lambda b, pt: (0, b, 0)),
                  pl.BlockSpec((nq, KV_LORA), lambda b, pt: (b, 0)),
                  pl.BlockSpec((nq, DR), lambda b, pt: (b, 0)),
                  pl.BlockSpec(wuv.shape, lambda b, pt: (0, 0, 0)),
                  pl.BlockSpec(memory_space=pl.ANY),
                  pl.BlockSpec(memory_space=pl.ANY)],
        out_specs=pl.BlockSpec((nq, HB * DV), lambda b, pt: (b, 0)),
        scratch_shapes=[pltpu.VMEM((2, past, KV_LORA), F32),
                        pltpu.VMEM((2, DR, past), F32),
                        pltpu.SemaphoreType.DMA((2, 2)),
                        pltpu.VMEM((rows, past + LANE), F32)])
    return pl.pallas_call(
        functools.partial(_mla_s_kernel, layer=layer, n_pages=n_pages, nq=nq, scale=(DN + DR) ** -0.5),
        grid_spec=grid_spec,
        out_shape=jax.ShapeDtypeStruct((nb * nq, HB * DV), F32),
        compiler_params=_cparams(("arbitrary",)),
        name="mla_sample",
    )(page_table, qlat, qpe, blat, bkpe, wuv, cache_lat, cache_kpet)


def _diff_s_kernel(pt_ref, cq_ref, nck_ref, ncv_ref, bias_ref, bnew_ref, lq1_ref, lk1_ref, lq2_ref, lk2_ref,
                   gsub_ref, ck_hbm, cv_hbm, o_ref, ckbuf, cvbuf, sem, s_ref, *, layer, n_pages, nq, lam_init):
    slot = _paged_pipeline(pt_ref, n_pages, layer, (ck_hbm, cv_hbm), (ckbuf, cvbuf),
                           (HC_KV * PAGE, HC_KV * PAGE), sem)
    past = n_pages * PAGE
    rows = 2 * HC * nq
    q = _diff_queries(cq_ref[...], nq)

    def packed(buf, ks):
        n = ks.stop - ks.start
        heads = [buf[slot, pl.ds(HC_KV * ks.start + kvh, n, stride=HC_KV), :] for kvh in range(HC_KV)]
        return jnp.concatenate(heads, axis=1).astype(BF16)

    for ks in _key_chunks(past):
        s_ref[:, ks] = _dot_nt(q, packed(ckbuf, ks)) + bias_ref[:, ks]
    nk = _pad_rows(nck_ref[...], LANE).astype(BF16)
    nv = _pad_rows(ncv_ref[...], LANE).astype(BF16)
    s_ref[:, past:] = jnp.where(_new_key_mask(rows, nq), _dot_nt(q, nk) + bnew_ref[...], NEG)

    def values(ks):
        return nv if ks is None else packed(cvbuf, ks)

    o = _softmax_pv(s_ref, past, values, lambda p, v: _diff_pv(p, v, HC * nq))
    lam = _diff_lambda(lq1_ref, lk1_ref, lq2_ref, lk2_ref, lam_init)
    _diff_finish(o, nq, lam, gsub_ref[...], lam_init, o_ref)


def diff_sample(page_table, h, bias, bias_new, lams, gsub, cache_ck, cache_cv, layer, nq, lam_init):
    nb, n_pages = page_table.shape
    past = n_pages * PAGE
    rows = 2 * HC * nq
    vec = pl.BlockSpec((1, DC_QK), lambda b, pt: (0, 0))
    grid_spec = pltpu.PrefetchScalarGridSpec(
        num_scalar_prefetch=1,
        grid=(nb,),
        in_specs=[pl.BlockSpec((nq, HC * LANE), lambda b, pt: (b, C_CQ // (HC * LANE))),
                  pl.BlockSpec((nq, HC_KV * LANE), lambda b, pt: (b, C_CK // (HC_KV * LANE))),
                  pl.BlockSpec((nq, HC_KV * DC_V), lambda b, pt: (b, C_CV // (HC_KV * DC_V))),
                  pl.BlockSpec(bias.shape, lambda b, pt: (0, 0)),
                  pl.BlockSpec(bias_new.shape, lambda b, pt: (0, 0)),
                  vec, vec, vec, vec,
                  pl.BlockSpec((1, DC_V), lambda b, pt: (0, 0)),
                  pl.BlockSpec(memory_space=pl.ANY),
                  pl.BlockSpec(memory_space=pl.ANY)],
        out_specs=pl.BlockSpec((nq, HC * DC_V), lambda b, pt: (b, 0)),
        scratch_shapes=[pltpu.VMEM((2, past * HC_KV, LANE), F32),
                        pltpu.VMEM((2, past * HC_KV, DC_V), F32),
                        pltpu.SemaphoreType.DMA((2, 2)),
                        pltpu.VMEM((rows, past + LANE), F32)])
    return pl.pallas_call(
        functools.partial(_diff_s_kernel, layer=layer, n_pages=n_pages, nq=nq, lam_init=lam_init),
        grid_spec=grid_spec,
        out_shape=jax.ShapeDtypeStruct((nb * nq, HC * DC_V), F32),
        compiler_params=_cparams(("arbitrary",)),
        name="diff_sample",
    )(page_table, h, h, h, bias, bias_new, *lams, gsub.reshape(1, -1), cache_ck, cache_cv)


def _dsa_sel_kernel(pt_ref, iq_ref, iw_ref, nik_ref, kidxt_hbm, o_ref, kidxtbuf, sem, sc_ref,
                    *, layer, n_pages, nq, topk, bb):
    slot = _paged_pipeline(pt_ref, n_pages, layer, (kidxt_hbm,), (kidxtbuf,), (0,), sem, bb)
    past = n_pages * PAGE
    for rr in range(bb):
        rs = slice(rr * nq, (rr + 1) * nq)
        iq = iq_ref[rs, :]
        iw = iw_ref[rs, :] * (D_IDX ** -0.5 * H_IDX ** -0.5)
        xq = jnp.concatenate([iq[:, h * D_IDX:(h + 1) * D_IDX] for h in range(H_IDX)], axis=0).astype(BF16)
        iwc = jnp.concatenate([iw[:, h:h + 1] for h in range(H_IDX)], axis=0)

        def scores(r, iwc=iwc):
            return jnp.sum((jnp.maximum(r, 0.0) * iwc).reshape(H_IDX, nq, r.shape[1]), axis=0)

        lead = (slot,) if bb == 1 else (slot, rr)
        for ks in _key_chunks(past):
            sc_ref[rs, ks] = scores(_dot(xq, kidxtbuf[lead + (slice(None), ks)].astype(BF16)))
        s_new = scores(_dot_nt(xq, _pad_rows(nik_ref[rs, :][:, :D_IDX], LANE).astype(BF16)))
        sc_ref[rs, past:] = jnp.where(_new_key_mask(nq, nq), s_new, -jnp.inf)

    idx = lax.broadcasted_iota(I32, (1, past + LANE), 1)
    o_ref[...] = _select_topk(sc_ref[...], idx, topk, (1,), (past + LANE - 1).bit_length(), 1)


def dsa_select_sample(page_table, h, cache_kidxt, layer, nq, topk):
    nb, n_pages = page_table.shape
    past = n_pages * PAGE
    bb = min(nb, 8)
    assert nb % bb == 0
    rows = bb * nq
    grid_spec = pltpu.PrefetchScalarGridSpec(
        num_scalar_prefetch=1,
        grid=(nb // bb,),
        in_specs=[pl.BlockSpec((rows, H_IDX * D_IDX), lambda g, pt: (g, C_IQ // (H_IDX * D_IDX))),
                  pl.BlockSpec((rows, LANE), lambda g, pt: (g, C_IW // LANE)),
                  pl.BlockSpec((rows, LANE), lambda g, pt: (g, C_IK2 // LANE)),
                  pl.BlockSpec(memory_space=pl.ANY)],
        out_specs=pl.BlockSpec((rows, past + LANE), lambda g, pt: (g, 0)),
        scratch_shapes=[pltpu.VMEM((2, D_IDX, past) if bb == 1 else (2, bb, D_IDX, past), F32),
                        pltpu.SemaphoreType.DMA((1, 2)),
                        pltpu.VMEM((rows, past + LANE), F32)])
    return pl.pallas_call(
        functools.partial(_dsa_sel_kernel, layer=layer, n_pages=n_pages, nq=nq, topk=topk, bb=bb),
        grid_spec=grid_spec,
        out_shape=jax.ShapeDtypeStruct((nb * nq, past + LANE), F32),
        compiler_params=_cparams(("arbitrary",)),
        name="dsa_select",
    )(page_table, h, h, h, cache_kidxt)


def _dsa_s_kernel(pt_ref, aq_ref, sb_ref, nak_ref, nav_ref, bias_ref, bnew_ref,
                  ak_hbm, av_hbm, o_ref, akbuf, avbuf, sem, s_ref, *, layer, n_pages, nq):
    slot = _paged_pipeline(pt_ref, n_pages, layer, (ak_hbm, av_hbm), (akbuf, avbuf), (PAGE, PAGE), sem)
    past = n_pages * PAGE
    sb = sb_ref[...]
    aq = aq_ref[...] * (DA ** -0.5)
    q = jnp.concatenate([aq[:, h * DA:(h + 1) * DA] for h in range(HA)], axis=0).astype(BF16)
    rows = HA * nq

    def logits(k, bias, sbc):
        return ((_dot_nt(q, k)).reshape(HA, nq, k.shape[0]) + bias + sbc[None]).reshape(rows, k.shape[0])

    for ks in _key_chunks(past):
        s_ref[:, ks] = logits(akbuf[slot, ks, :].astype(BF16), bias_ref[:, :, ks], sb[:, ks])
    s_ref[:, past:] = logits(_pad_rows(nak_ref[...], LANE).astype(BF16), bnew_ref[...], sb[:, past:])
    nv = _pad_rows(nav_ref[...], LANE).astype(BF16)

    def values(ks):
        return nv if ks is None else avbuf[slot, ks, :].astype(BF16)

    o = _softmax_pv(s_ref, past, values, _dot)
    for h in range(HA):
        o_ref[:, h * DA:(h + 1) * DA] = o[h * nq:(h + 1) * nq]


def dsa_sample(page_table, h, sel, bias, bias_new, cache_ak, cache_av, layer, nq):
    nb, n_pages = page_table.shape
    past = n_pages * PAGE
    rows = HA * nq
    grid_spec = pltpu.PrefetchScalarGridSpec(
        num_scalar_prefetch=1,
        grid=(nb,),
        in_specs=[pl.BlockSpec((nq, HA * DA), lambda b, pt: (b, C_AQ // (HA * DA))),
                  pl.BlockSpec((nq, past + LANE), lambda b, pt: (b, 0)),
                  pl.BlockSpec((nq, DA), lambda b, pt: (b, C_AK // DA)),
                  pl.BlockSpec((nq, DA), lambda b, pt: (b, C_AV // DA)),
                  pl.BlockSpec(bias.shape, lambda b, pt: (0, 0, 0)),
                  pl.BlockSpec(bias_new.shape, lambda b, pt: (0, 0, 0)),
                  pl.BlockSpec(memory_space=pl.ANY),
                  pl.BlockSpec(memory_space=pl.ANY)],
        out_specs=pl.BlockSpec((nq, HA * DA), lambda b, pt: (b, 0)),
        scratch_shapes=[pltpu.VMEM((2, past, DA), F32),
                        pltpu.VMEM((2, past, DA), F32),
                        pltpu.SemaphoreType.DMA((2, 2)),
                        pltpu.VMEM((rows, past + LANE), F32)])
    return pl.pallas_call(
        functools.partial(_dsa_s_kernel, layer=layer, n_pages=n_pages, nq=nq),
        grid_spec=grid_spec,
        out_shape=jax.ShapeDtypeStruct((nb * nq, HA * DA), F32),
        compiler_params=_cparams(("arbitrary",)),
        name="dsa_sample",
    )(page_table, h, sel, h, h, bias, bias_new, cache_ak, cache_av)


def _out_kernel(a_ref, b_ref, c_ref, x_ref, w_ref, g_ref, o_ref):
    na = HA * HEAD_DIM
    nab = na + HB * DV
    y = (_dot(a_ref[...].astype(BF16), w_ref[:na, :])
         + _dot(b_ref[...].astype(BF16), w_ref[na:nab, :])
         + _dot(c_ref[...].astype(BF16), w_ref[nab:, :]))
    o_ref[...] = x_ref[...] + _rms(y, g_ref[...])


def out_project(a, b, c, x, w, g):
    rows = x.shape[0]
    tm = _row_tile(rows, 256)
    row = lambda i: (i, 0)
    return pl.pallas_call(
        _out_kernel,
        grid=(rows // tm,),
        in_specs=[pl.BlockSpec((tm, a.shape[1]), row), pl.BlockSpec((tm, b.shape[1]), row),
                  pl.BlockSpec((tm, c.shape[1]), row), pl.BlockSpec((tm, D_MODEL), row),
                  pl.BlockSpec(w.shape, lambda i: (0, 0)), pl.BlockSpec((1, D_MODEL), lambda i: (0, 0))],
        out_specs=pl.BlockSpec((tm, D_MODEL), row),
        out_shape=jax.ShapeDtypeStruct((rows, D_MODEL), F32),
        compiler_params=_cparams(("parallel",)),
        name="out_proj",
    )(a, b, c, x, w, g.reshape(1, -1))


def _ffn_kernel(x_ref, gpre_ref, wg_ref, wu_ref, wd_ref, gpost_ref, o_ref, xn_ref, acc_ref):
    j = pl.program_id(1)

    @pl.when(j == 0)
    def _():
        xn_ref[...] = _rms(x_ref[...], gpre_ref[...]).astype(BF16)
        acc_ref[...] = jnp.zeros(acc_ref.shape, F32)

    xn = xn_ref[...]
    a = _dot(xn, wg_ref[...])
    u = _dot(xn, wu_ref[...])
    hmid = (a * (1.0 / (1.0 + jnp.exp(-a)))) * u
    acc_ref[...] += _dot(hmid.astype(BF16), wd_ref[...])

    @pl.when(j == pl.num_programs(1) - 1)
    def _():
        o_ref[...] = x_ref[...] + _rms(acc_ref[...], gpost_ref[...])


def ffn(x, g_pre, wg, wu, wd, g_post, tf):
    rows = x.shape[0]
    dff = wg.shape[1]
    tm = _row_tile(rows, 512)
    return pl.pallas_call(
        _ffn_kernel,
        grid=(rows // tm, dff // tf),
        in_specs=[pl.BlockSpec((tm, D_MODEL), lambda i, j: (i, 0)),
                  pl.BlockSpec((1, D_MODEL), lambda i, j: (0, 0)),
                  pl.BlockSpec((D_MODEL, tf), lambda i, j: (0, j)),
                  pl.BlockSpec((D_MODEL, tf), lambda i, j: (0, j)),
                  pl.BlockSpec((tf, D_MODEL), lambda i, j: (j, 0)),
                  pl.BlockSpec((1, D_MODEL), lambda i, j: (0, 0))],
        out_specs=pl.BlockSpec((tm, D_MODEL), lambda i, j: (i, 0)),
        out_shape=jax.ShapeDtypeStruct((rows, D_MODEL), F32),
        scratch_shapes=[pltpu.VMEM((tm, D_MODEL), BF16), pltpu.VMEM((tm, D_MODEL), F32)],
        compiler_params=_cparams(("parallel", "arbitrary")),
        name="ffn",
    )(x, g_pre.reshape(1, -1), wg, wu, wd, g_post.reshape(1, -1))


def _rope_tables(pos, rows):
    half = DR // 2
    inv = jnp.power(ROPE_THETA, -jnp.arange(half, dtype=F32) / half)
    ang = pos.astype(F32)[:, None] * inv[None, :]
    cos, sin = jnp.cos(ang), jnp.sin(ang)
    c = jnp.tile(jnp.concatenate([cos, cos], axis=-1), (rows // pos.shape[0], HB))
    s = jnp.tile(jnp.concatenate([-sin, sin], axis=-1), (rows // pos.shape[0], HB))
    return c, s


def _prompt_buckets(tq, tk):
    i = np.arange(tq)[None, :]
    j = np.arange(tk)[:, None]
    zones = [i - j + z * tq for z in range(tk // tq + 1)] + [np.full((tk, tq), MAX_DISTANCE, np.int64)]
    return _bucket_np(np.concatenate(zones, axis=0))


def _sample_buckets(nq, past):
    qs = np.arange(nq)[:, None]
    j = np.arange(PAGE)[None, :]
    last = (past + qs) - (past - PAGE + j)
    new = np.maximum(qs - j, 0)
    return _bucket_np(np.concatenate([last, new], axis=0))


def kernel(x_prompt, x_sample, cache_a_k, cache_a_v, cache_a_kidx, cache_b_latent, cache_b_krope, cache_c_k, cache_c_v, page_table, rel_bias, norm_pre_mix, w_in, q_a_norm, w_q_b, kv_a_norm, w_uk, w_uv, lambda_q1, lambda_k1, lambda_q2, lambda_k2, c_subln, w_out, norm_post_mix, norm_pre_ffn, w_gate, w_up, w_down, norm_post_ffn):
    nb_p, seq, _ = x_prompt.shape
    nb_s, nq_s, _ = x_sample.shape
    depth = w_in.shape[0]
    n_pages = page_table.shape[1]
    past = n_pages * PAGE
    topk_p = min(TOPK_MAX, seq // 4)
    topk_s = min(TOPK_MAX, (past + nq_s) // 4)
    tq = TQ
    tk = min(TK, seq)
    assert seq % tk == 0 and tk % tq == 0 and nq_s == 8

    perm = _in_perm()
    w_in_r = (jnp.take(w_in, jnp.asarray(np.maximum(perm, 0)), axis=2)
              * jnp.asarray(perm >= 0, F32)).astype(BF16)
    w_qb_r = jnp.take(w_q_b, jnp.asarray(_qb_perm()), axis=2).astype(BF16)
    w_uk_r = jnp.transpose(w_uk, (0, 2, 3, 1)).astype(BF16)
    w_uv_r = jnp.transpose(w_uv, (0, 2, 1, 3)).astype(BF16)
    w_uv_t = jnp.transpose(w_uv, (0, 2, 3, 1)).astype(BF16)
    w_out_b = w_out.astype(BF16)
    w_gate_b = w_gate.astype(BF16)
    w_up_b = w_up.astype(BF16)
    w_down_b = w_down.astype(BF16)
    cache_kidxt = jnp.swapaxes(cache_a_kidx, 2, 3)
    cache_kpet = jnp.swapaxes(cache_b_krope, 2, 3)
    cache_ck2 = cache_c_k.reshape(cache_c_k.shape[:2] + (PAGE * HC_KV, 2 * DC_QK))
    cache_cv2 = cache_c_v.reshape(cache_c_v.shape[:2] + (PAGE * HC_KV, DC_V))

    rows_p = nb_p * seq
    rows_s = nb_s * nq_s
    tm_p = _row_tile(rows_p, 512)
    tm_s = _row_tile(rows_s, 512)
    cos_p, sin_p = _rope_tables(jnp.arange(seq, dtype=I32), max(seq, tm_p))
    cos_s, sin_s = _rope_tables(past + jnp.arange(nq_s, dtype=I32), max(nq_s, tm_s))

    nz = tk // tq + 2
    zones = t5_expand(rel_bias, jnp.asarray(_prompt_buckets(tq, tk)), tk).reshape(HA + HC, nz, tk, tq)
    zones = jnp.transpose(zones, (1, 0, 2, 3))
    bias_a_p, bias_c_p = zones[:, :HA], zones[:, HA:]
    tiles = t5_expand(rel_bias, jnp.asarray(_sample_buckets(nq_s, past)), 2 * nq_s)
    last, new = tiles[:, :nq_s], tiles[:, nq_s:]
    far = jnp.broadcast_to(zones[nz - 1, :, :1, :1], (HA + HC, nq_s, past - PAGE))
    full = jnp.concatenate([far, last], axis=-1)
    bias_a_s, bias_a_new = full[:HA], new[:HA]
    rep = lambda t: jnp.repeat(t[HA:], 2, axis=0).reshape(2 * HC * nq_s, t.shape[-1])
    bias_c_s, bias_c_new = rep(full), rep(new)

    hp = x_prompt.reshape(rows_p, D_MODEL)
    hs = x_sample.reshape(rows_s, D_MODEL)
    new_p = [[] for _ in range(7)]
    new_s = [[] for _ in range(7)]
    for l in range(depth):
        lam_init = 0.8 - 0.6 * math.exp(-0.3 * l)
        lams = tuple(v[l].reshape(1, DC_QK) for v in (lambda_q1, lambda_k1, lambda_q2, lambda_k2))
        outs = []
        for grp, (x, cos_t, sin_t) in enumerate(((hp, cos_p, sin_p), (hs, cos_s, sin_s))):
            h = norm_matmul(x, norm_pre_mix[l], w_in_r[l], IN_COLS_R // 3)
            qlat, qpe, blat, bkpe, *vts = post_project(h, cos_t, sin_t, q_a_norm[l], kv_a_norm[l], w_qb_r[l],
                                                       w_uk_r[l], grp == 0)
            if grp == 0:
                latt, avt, cvt = vts
                a_o = dsa_prompt(h, avt, bias_a_p, nb_p, seq, tk, topk_p)
                b_o = mla_prompt(qlat, qpe, blat, bkpe, latt, w_uv_t[l], nb_p, seq, tq, tk)
                c_o = diff_prompt(h, cvt, bias_c_p, lams, c_subln[l], nb_p, seq, tq, tk, lam_init)
            else:
                sel = dsa_select_sample(page_table, h, cache_kidxt, l, nq_s, topk_s)
                a_o = dsa_sample(page_table, h, sel, bias_a_s, bias_a_new, cache_a_k, cache_a_v, l, nq_s)
                b_o = mla_sample(page_table, qlat, qpe, blat, bkpe, w_uv_r[l], cache_b_latent, cache_kpet,
                                 l, nq_s)
                c_o = diff_sample(page_table, h, bias_c_s, bias_c_new, lams, c_subln[l], cache_ck2, cache_cv2,
                                  l, nq_s, lam_init)
            x1 = out_project(a_o, b_o, c_o, x, w_out_b[l], norm_post_mix[l])
            outs.append(ffn(x1, norm_pre_ffn[l], w_gate_b[l], w_up_b[l], w_down_b[l], norm_post_ffn[l], 512))
            rows = (h[:, C_AK:C_AK + DA], h[:, C_AV:C_AV + DA], h[:, C_IK2:C_IK2 + D_IDX], blat, bkpe,
                    h[:, C_CK:C_CK + HC_KV * LANE], h[:, C_CV:C_CV + HC_KV * DC_V])
            for i in range(7):
                (new_p if grp == 0 else new_s)[i].append(rows[i])
        hp, hs = outs

    def stack(parts, lead, tail):
        return jnp.stack(parts).reshape((depth,) + lead + tail)

    tails = ((DA,), (DA,), (D_IDX,), (KV_LORA,), (DR,), (HC_KV, 2 * DC_QK), (HC_KV, DC_V))
    outs_p = tuple(stack(new_p[i], (nb_p, seq), tails[i]) for i in range(7))
    outs_s = tuple(stack(new_s[i], (nb_s, nq_s), tails[i]) for i in range(7))
    return (hp.reshape(x_prompt.shape), hs.reshape(x_sample.shape)) + outs_p + outs_s
```

```python
import functools
import math

import numpy as np
import jax
import jax.numpy as jnp
from jax import lax
from jax.experimental import pallas as pl
from jax.experimental.pallas import tpu as pltpu

F32 = jnp.float32
BF16 = jnp.bfloat16
I32 = jnp.int32

D_MODEL = 2048
HEAD_DIM = 128
HA = 4
HC = 4
HB = 8
DA = 128
H_IDX = 8
D_IDX = 64
TOPK_MAX = 256
Q_LORA = 512
KV_LORA = 256
DN = 128
DR = 64
DV = 128
HC_KV = 2
C_GROUP = 2
DC_QK = 64
DC_V = 128
N_BUCKETS = 32
MAX_DISTANCE = 128
ROPE_THETA = 10000.0
EPS = 1e-6
PAGE = 128

LANE = 128
NEG = -1e30
INT_MIN = -2 ** 31
INT_MAX = 2 ** 31 - 1
NINF_KEY = -2139095041
VMEM_LIMIT = 52 * 1024 * 1024
TQ = 128
TK = 512
CW = 2048
MLA_SPLIT = 2

C_AQ, C_IQ, C_BQ, C_CQ = 0, 512, 1024, 1536
C_LAT, C_CK, C_CV = 2048, 2304, 2560
C_AK, C_AV, C_IK2, C_KPE2, C_IW = 2816, 2944, 3072, 3200, 3328
IN_COLS_R = 3456
_O_AQ, _O_AK, _O_AV, _O_IQ, _O_IK, _O_IW, _O_BQ, _O_BKV, _O_CQ, _O_CK, _O_CV = (
    0, 512, 640, 768, 1280, 1344, 1352, 1864, 2184, 2696, 2952)


def _in_perm():
    perm = np.full((IN_COLS_R,), -1, np.int64)
    def put(dst, src, n):
        perm[dst:dst + n] = np.arange(src, src + n)
    put(C_AQ, _O_AQ, 512)
    put(C_IQ, _O_IQ, 512)
    put(C_BQ, _O_BQ, 512)
    put(C_CQ, _O_CQ, 512)
    put(C_LAT, _O_BKV, 256)
    put(C_CK, _O_CK, 256)
    put(C_CV, _O_CV, 256)
    put(C_AK, _O_AK, 128)
    put(C_AV, _O_AV, 128)
    put(C_IK2, _O_IK, 64)
    put(C_IK2 + 64, _O_IK, 64)
    kpe = _O_BKV + KV_LORA
    put(C_KPE2, kpe, 64)
    put(C_KPE2 + 64, kpe + 32, 32)
    put(C_KPE2 + 96, kpe, 32)
    put(C_IW, _O_IW, 8)
    return perm


def _qb_perm():
    nope, pe, pes = [], [], []
    for h in range(HB):
        base = h * (DN + DR)
        nope += list(range(base, base + DN))
        pe += list(range(base + DN, base + DN + DR))
        pes += list(range(base + DN + 32, base + DN + 64)) + list(range(base + DN, base + DN + 32))
    return np.array(nope + pe + pes, np.int64)


def _reorder_cols(w, perm):
    pieces = []
    i = 0
    while i < len(perm):
        j = i + 1
        if perm[i] < 0:
            while j < len(perm) and perm[j] < 0:
                j += 1
            pieces.append(jnp.zeros(w.shape[:-1] + (j - i,), w.dtype))
        else:
            while j < len(perm) and perm[j] == perm[j - 1] + 1:
                j += 1
            pieces.append(w[..., int(perm[i]):int(perm[i]) + (j - i)])
        i = j
    return jnp.concatenate(pieces, axis=-1)


def _dot(a, b):
    return jnp.dot(a, b, preferred_element_type=F32)


def _dot_nt(a, b):
    return lax.dot_general(a, b, (((1,), (1,)), ((), ())), preferred_element_type=F32)


def _rms(x, g):
    return (x * lax.rsqrt(jnp.mean(x * x, axis=-1, keepdims=True) + EPS)) * g


def _cparams(sem):
    return pltpu.CompilerParams(dimension_semantics=sem, vmem_limit_bytes=VMEM_LIMIT)


def _row_tile(rows, want):
    t = min(rows, want)
    assert rows % t == 0, (rows, t)
    return t


def _norm_matmul_kernel(x_ref, g_ref, w_ref, o_ref, xn_ref):
    @pl.when(pl.program_id(1) == 0)
    def _():
        xn_ref[...] = _rms(x_ref[...], g_ref[...]).astype(BF16)
    o_ref[...] = _dot(xn_ref[...], w_ref[...])


def norm_matmul(x, g, w, layer, tn):
    m, k = x.shape
    n = w.shape[2]
    tm = _row_tile(m, 1024)
    return pl.pallas_call(
        _norm_matmul_kernel,
        grid=(m // tm, n // tn),
        in_specs=[pl.BlockSpec((tm, k), lambda i, j: (i, 0)),
                  pl.BlockSpec((1, k), lambda i, j: (0, 0)),
                  pl.BlockSpec((None, k, tn), lambda i, j: (layer, 0, j))],
        out_specs=pl.BlockSpec((tm, tn), lambda i, j: (i, j)),
        out_shape=jax.ShapeDtypeStruct((m, n), F32),
        scratch_shapes=[pltpu.VMEM((tm, k), BF16)],
        compiler_params=_cparams(("parallel", "arbitrary")),
        name="in_proj",
    )(x, g.reshape(1, k), w)


def _post_kernel(bq_ref, lat_ref, kpe2_ref, cos_ref, sin_ref, gq_ref, gkv_ref, wqb_ref, wuk_ref, *rest,
                 with_t):
    if with_t:
        av_ref, cv_ref, qlat_ref, qpe_ref, blat_ref, bkpe_ref, latt_ref, avt_ref, cvt_ref = rest
        blat = _rms(lat_ref[...], gkv_ref[...])
        latt_ref[0] = jnp.transpose(blat).astype(BF16)
        avt_ref[0] = jnp.transpose(av_ref[...]).astype(BF16)
        cvt_ref[0] = jnp.transpose(cv_ref[...]).astype(BF16)
    else:
        qlat_ref, qpe_ref, blat_ref, bkpe_ref = rest
    qn = _rms(bq_ref[...], gq_ref[...]).astype(BF16)
    qb = _dot(qn, wqb_ref[...])
    cos = cos_ref[...]
    sin = sin_ref[...]
    for h in range(HB):
        qlat_ref[h] = _dot(qb[:, h * DN:(h + 1) * DN].astype(BF16), wuk_ref[h])
    nq = HB * DN
    pe = qb[:, nq:nq + HB * DR] * cos + qb[:, nq + HB * DR:] * sin
    for h in range(HB):
        qpe_ref[h] = pe[:, h * DR:(h + 1) * DR]
    blat_ref[...] = _rms(lat_ref[...], gkv_ref[...])
    k2 = kpe2_ref[...]
    bkpe_ref[...] = k2[:, :DR] * cos[:, :DR] + k2[:, DR:] * sin[:, :DR]


def post_project(h, cos_t, sin_t, g_qa, g_kva, wqb, wuk, layer, with_t):
    rows = h.shape[0]
    tm = _row_tile(rows, 512)
    nt = cos_t.shape[0] // tm
    in_specs = [pl.BlockSpec((tm, Q_LORA), lambda i: (i, C_BQ // Q_LORA)),
                pl.BlockSpec((tm, KV_LORA), lambda i: (i, C_LAT // KV_LORA)),
                pl.BlockSpec((tm, LANE), lambda i: (i, C_KPE2 // LANE)),
                pl.BlockSpec((tm, HB * DR), lambda i: (i % nt, 0)),
                pl.BlockSpec((tm, HB * DR), lambda i: (i % nt, 0)),
                pl.BlockSpec((1, Q_LORA), lambda i: (0, 0)),
                pl.BlockSpec((1, KV_LORA), lambda i: (0, 0)),
                pl.BlockSpec((None,) + wqb.shape[1:], lambda i: (layer, 0, 0)),
                pl.BlockSpec((None,) + wuk.shape[1:], lambda i: (layer, 0, 0, 0))]
    out_specs = [pl.BlockSpec((HB, tm, KV_LORA), lambda i: (0, i, 0)),
                 pl.BlockSpec((HB, tm, DR), lambda i: (0, i, 0)),
                 pl.BlockSpec((tm, KV_LORA), lambda i: (i, 0)),
                 pl.BlockSpec((tm, DR), lambda i: (i, 0))]
    out_shape = [jax.ShapeDtypeStruct((HB, rows, KV_LORA), F32),
                 jax.ShapeDtypeStruct((HB, rows, DR), F32),
                 jax.ShapeDtypeStruct((rows, KV_LORA), F32),
                 jax.ShapeDtypeStruct((rows, DR), F32)]
    args = [h, h, h, cos_t, sin_t, g_qa.reshape(1, -1), g_kva.reshape(1, -1), wqb, wuk]
    if with_t:
        in_specs += [pl.BlockSpec((tm, DA), lambda i: (i, C_AV // DA)),
                     pl.BlockSpec((tm, HC_KV * DC_V), lambda i: (i, C_CV // (HC_KV * DC_V)))]
        args += [h, h]
        for d in (KV_LORA, DA, HC_KV * DC_V):
            out_specs.append(pl.BlockSpec((1, d, tm), lambda i: (i, 0, 0)))
            out_shape.append(jax.ShapeDtypeStruct((rows // tm, d, tm), BF16))
    return pl.pallas_call(
        functools.partial(_post_kernel, with_t=with_t),
        grid=(rows // tm,),
        in_specs=in_specs,
        out_specs=out_specs,
        out_shape=out_shape,
        compiler_params=_cparams(("parallel",)),
        name="post_proj",
    )(*args)


def _bucket_np(n):
    n = np.maximum(n, 0)
    half = N_BUCKETS // 2
    nf = np.maximum(n, 1).astype(np.float32)
    large = half + (np.log(nf / np.float32(half)) / np.float32(math.log(MAX_DISTANCE / half))
                    * np.float32(N_BUCKETS - half)).astype(np.int32)
    return np.where(n < half, n, np.minimum(large, N_BUCKETS - 1)).astype(np.int32)


def _t5_kernel(tab_ref, bkt_ref, o_ref):
    bkt = bkt_ref[...]
    for h in range(HA + HC):
        acc = jnp.zeros(bkt.shape, F32)
        for b in range(N_BUCKETS):
            acc = jnp.where(bkt == b, tab_ref[b, h], acc)
        o_ref[h] = acc


def t5_expand(table, buckets, tr):
    r, c = buckets.shape
    return pl.pallas_call(
        _t5_kernel,
        grid=(r // tr,),
        in_specs=[pl.BlockSpec(memory_space=pltpu.SMEM),
                  pl.BlockSpec((tr, c), lambda i: (i, 0))],
        out_specs=pl.BlockSpec((HA + HC, tr, c), lambda i: (0, i, 0)),
        out_shape=jax.ShapeDtypeStruct((HA + HC, r, c), F32),
        name="t5_bias",
    )(table, buckets)


def _flash_update(s, pv, m_ref, l_ref, acc_ref, rs=slice(None)):
    m = m_ref[rs, :]
    m_new = jnp.maximum(m, jnp.max(s, axis=-1, keepdims=True))
    alpha = jnp.exp(m - m_new)
    p = jnp.exp(s - m_new)
    l_ref[rs, :] = alpha * l_ref[rs, :] + jnp.sum(p, axis=-1, keepdims=True)
    acc_ref[rs, :] = alpha * acc_ref[rs, :] + pv(p.astype(BF16))
    m_ref[rs, :] = m_new


def _flash_update_t(st, pv, m_ref, l_ref, acc_ref):
    m = m_ref[...]
    m_new = jnp.maximum(m, jnp.max(st, axis=0, keepdims=True))
    alpha = jnp.exp(m - m_new)
    p = jnp.exp(st - m_new)
    l_ref[...] = alpha * l_ref[...] + jnp.sum(p, axis=0, keepdims=True)
    acc_ref[...] = alpha * acc_ref[...] + pv(p.astype(BF16))
    m_ref[...] = m_new


def _flash_init(m_ref, l_ref, acc_ref):
    m_ref[...] = jnp.full(m_ref.shape, NEG, F32)
    l_ref[...] = jnp.zeros(l_ref.shape, F32)
    acc_ref[...] = jnp.zeros(acc_ref.shape, F32)


def _count(mask, axes):
    c = jnp.where(mask, 1.0, 0.0)
    for ax in axes:
        c = jnp.sum(c, axis=ax, keepdims=True)
    return c


def _select_topk(sc, idx, topk, axes, nbits, radix_bits):
    sc = jnp.where(sc == 0.0, 0.0, sc)
    bits = pltpu.bitcast(sc, I32)
    key = bits ^ ((bits >> 31) & 0x7FFFFFFF)
    kf = float(topk)
    tshape = tuple(1 if a in axes else d for a, d in enumerate(sc.shape))

    def tbody(i, t):
        shift = 32 - radix_bits * (i + 1)
        new = t
        for v in range(1, 2 ** radix_bits):
            cand = t + jnp.left_shift(jnp.int32(v), shift)
            new = jnp.where(_count(key >= cand, axes) >= kf, cand, new)
        return new

    t = lax.fori_loop(0, 32 // radix_bits, tbody, jnp.full(tshape, INT_MIN, I32))
    gt = key > t
    eq = key == t
    need = kf - _count(gt, axes)
    spare = _count(eq, axes) - need
    contested = jnp.max(jnp.where(t > NINF_KEY, spare, 0.0))

    def last_tie():
        def jbody(i, j):
            cand = j + jnp.left_shift(jnp.int32(1), nbits - 1 - i)
            c = jnp.where(eq, jnp.where(idx < cand, 1.0, 0.0), 0.0)
            for ax in axes:
                c = jnp.sum(c, axis=ax, keepdims=True)
            return jnp.where(c < need, cand, j)
        return lax.fori_loop(0, nbits, jbody, jnp.zeros(tshape, I32))

    jmax = lax.cond(contested > 0.0, last_tie, lambda: jnp.full(tshape, INT_MAX, I32))
    take = jnp.where(gt, 1.0, jnp.where(eq, jnp.where(idx <= jmax, 1.0, 0.0), 0.0))
    take = jnp.where(sc > -jnp.inf, take, 0.0)
    return jnp.where(take > 0.5, 0.0, NEG)


def _split_maps(x):
    lo = lax.broadcasted_iota(I32, (1, LANE), 1) < LANE // 2
    return jnp.where(lo, x, 0.0), jnp.where(lo, 0.0, x)


def _diff_lambda(lq1_ref, lk1_ref, lq2_ref, lk2_ref, lam_init):
    a = jnp.sum(lq1_ref[...] * lk1_ref[...], axis=-1, keepdims=True)
    b = jnp.sum(lq2_ref[...] * lk2_ref[...], axis=-1, keepdims=True)
    return jnp.exp(a) - jnp.exp(b) + lam_init


def _diff_queries(x, rows):
    zero = jnp.zeros((rows, LANE), F32)
    parts = []
    for hc in range(HC):
        for piece in _split_maps(x[:, hc * LANE:(hc + 1) * LANE] * (DC_QK ** -0.5)):
            parts.append(jnp.concatenate([piece, zero] if hc < C_GROUP else [zero, piece], axis=1))
    return jnp.concatenate(parts, axis=0).astype(BF16)


def _diff_pv(p, v, half_rows):
    return jnp.concatenate([_dot(p[:half_rows], v[:, :DC_V]), _dot(p[half_rows:], v[:, DC_V:])], axis=0)


def _diff_finish(o, rows, lam, gsub, lam_init, o_ref):
    for hc in range(HC):
        r0 = 2 * hc * rows
        d = o[r0:r0 + rows] - lam * o[r0 + rows:r0 + 2 * rows]
        o_ref[:, hc * DC_V:(hc + 1) * DC_V] = _rms(d, gsub) * (1.0 - lam_init)


def _zone(j, qi, r):
    return jnp.clip(qi - r * j, 0, r + 1)


def _causal_chunks(qi, r, step):
    last = qi // r

    def body(j, carry):
        step(j, False)
        return carry

    lax.fori_loop(0, last, body, 0)
    step(last, True)


def _key_major_masks(qi, tq, tk, groups):
    qpos = qi * tq + lax.broadcasted_iota(I32, (1, groups * tq), 1) % tq
    krow = lax.broadcasted_iota(I32, (tk, 1), 0)
    return qpos, krow


def _mla_p_kernel(qlat_ref, qpe_ref, klat_ref, kpe_ref, vt_ref, wuvt_ref, o_ref, m_ref, l_ref, acc_ref,
                  *, tq, tk, scale):
    qi = pl.program_id(1)
    ql = (qlat_ref[...] * scale).reshape(HB * tq, KV_LORA).astype(BF16)
    qp = (qpe_ref[...] * scale).reshape(HB * tq, DR).astype(BF16)
    qpos, krow = _key_major_masks(qi, tq, tk, HB)
    _flash_init(m_ref, l_ref, acc_ref)

    def step(j, masked):
        k0 = pl.multiple_of(j * tk, tk)
        kl = klat_ref[pl.ds(k0, tk), :].astype(BF16)
        kp = kpe_ref[pl.ds(k0, tk), :].astype(BF16)
        st = _dot_nt(kl, ql) + _dot_nt(kp, qp)
        if masked:
            st = jnp.where(k0 + krow <= qpos, st, NEG)
        _flash_update_t(st, lambda p: _dot(vt_ref[j], p), m_ref, l_ref, acc_ref)

    _causal_chunks(qi, tk // tq, step)
    ot = acc_ref[...] / l_ref[...]
    bot = [_dot(wuvt_ref[h], ot[:, h * tq:(h + 1) * tq].astype(BF16)) for h in range(HB)]
    o_ref[...] = jnp.transpose(jnp.concatenate(bot, axis=0))


def mla_prompt(qlat, qpe, blat, bkpe, latt, wuvt, layer, nb, seq, tq, tk):
    nq = seq // tq
    rows = HB * tq
    assert latt.shape[2] == tk
    return pl.pallas_call(
        functools.partial(_mla_p_kernel, tq=tq, tk=tk, scale=(DN + DR) ** -0.5),
        grid=(nb, nq),
        in_specs=[pl.BlockSpec((HB, tq, KV_LORA), lambda b, i: (0, b * nq + i, 0)),
                  pl.BlockSpec((HB, tq, DR), lambda b, i: (0, b * nq + i, 0)),
                  pl.BlockSpec((seq, KV_LORA), lambda b, i: (b, 0)),
                  pl.BlockSpec((seq, DR), lambda b, i: (b, 0)),
                  pl.BlockSpec((seq // tk, KV_LORA, tk), lambda b, i: (b, 0, 0)),
                  pl.BlockSpec((None,) + wuvt.shape[1:], lambda b, i: (layer, 0, 0, 0))],
        out_specs=pl.BlockSpec((tq, HB * DV), lambda b, i: (b * nq + i, 0)),
        out_shape=jax.ShapeDtypeStruct((nb * seq, HB * DV), F32),
        scratch_shapes=[pltpu.VMEM((1, rows), F32), pltpu.VMEM((1, rows), F32),
                        pltpu.VMEM((KV_LORA, rows), F32)],
        compiler_params=_cparams(("parallel", "arbitrary")),
        name="mla_prompt",
    )(qlat, qpe, blat, bkpe, latt, wuvt)


def _diff_p_kernel(cq_ref, ck_ref, cvt_ref, bias_ref, lq1_ref, lk1_ref, lq2_ref, lk2_ref, gsub_ref,
                   o_ref, m_ref, l_ref, acc_ref, *, tq, tk, lam_init):
    qi = pl.program_id(1)
    r = tk // tq
    q = _diff_queries(cq_ref[...], tq)
    qpos, krow = _key_major_masks(qi, tq, tk, 2 * HC)
    half = HC * tq
    _flash_init(m_ref, l_ref, acc_ref)

    def step(j, masked):
        k0 = pl.multiple_of(j * tk, tk)
        k = ck_ref[pl.ds(k0, tk), :].astype(BF16)
        st = _dot_nt(k, q)
        z = _zone(j, qi, r)
        st = jnp.concatenate([st[:, c * tq:(c + 1) * tq] + bias_ref[z, c // 2] for c in range(2 * HC)], axis=1)
        if masked:
            st = jnp.where(k0 + krow <= qpos, st, NEG)

        def pv(p):
            vt = cvt_ref[j]
            return jnp.concatenate([_dot(vt[:DC_V], p[:, :half]), _dot(vt[DC_V:], p[:, half:])], axis=1)

        _flash_update_t(st, pv, m_ref, l_ref, acc_ref)

    _causal_chunks(qi, r, step)
    lam = _diff_lambda(lq1_ref, lk1_ref, lq2_ref, lk2_ref, lam_init)
    ot = acc_ref[...] / l_ref[...]
    for hc in range(HC):
        c0 = 2 * hc * tq
        d = ot[:, c0:c0 + tq] - lam * ot[:, c0 + tq:c0 + 2 * tq]
        y = (d * lax.rsqrt(jnp.mean(d * d, axis=0, keepdims=True) + EPS)) * gsub_ref[...]
        o_ref[:, hc * DC_V:(hc + 1) * DC_V] = jnp.transpose(y * (1.0 - lam_init))


def diff_prompt(h, cvt, bias, lams, gsub, nb, seq, tq, tk, lam_init):
    nq = seq // tq
    rows = 2 * HC * tq
    assert cvt.shape[2] == tk
    vec = pl.BlockSpec((1, DC_QK), lambda b, i: (0, 0))
    return pl.pallas_call(
        functools.partial(_diff_p_kernel, tq=tq, tk=tk, lam_init=lam_init),
        grid=(nb, nq),
        in_specs=[pl.BlockSpec((tq, HC * LANE), lambda b, i: (b * nq + i, C_CQ // (HC * LANE))),
                  pl.BlockSpec((seq, HC_KV * LANE), lambda b, i: (b, C_CK // (HC_KV * LANE))),
                  pl.BlockSpec((seq // tk, HC_KV * DC_V, tk), lambda b, i: (b, 0, 0)),
                  pl.BlockSpec(bias.shape, lambda b, i: (0, 0, 0, 0)),
                  vec, vec, vec, vec,
                  pl.BlockSpec((DC_V, 1), lambda b, i: (0, 0))],
        out_specs=pl.BlockSpec((tq, HC * DC_V), lambda b, i: (b * nq + i, 0)),
        out_shape=jax.ShapeDtypeStruct((nb * seq, HC * DC_V), F32),
        scratch_shapes=[pltpu.VMEM((1, rows), F32), pltpu.VMEM((1, rows), F32),
                        pltpu.VMEM((DC_V, rows), F32)],
        compiler_params=_cparams(("parallel", "arbitrary")),
        name="diff_prompt",
    )(h, h, cvt, bias, *lams, gsub.reshape(-1, 1))


def _dsa_p_kernel(aq_ref, iq_ref, iw_ref, ik2_ref, ak_ref, avt_ref, bias_ref, o_ref,
                  sct_ref, sbt_ref, m_ref, l_ref, acc_ref, *, tq, tk, nc, topk):
    qi = pl.program_id(1)
    r = tk // tq
    iq = iq_ref[...]
    parts = []
    for p in range(H_IDX // 2):
        parts += list(_split_maps(iq[:, p * LANE:(p + 1) * LANE]))
    xq = jnp.concatenate(parts, axis=0).astype(BF16)
    iwt = jnp.transpose(iw_ref[...])[:H_IDX] * (D_IDX ** -0.5 * H_IDX ** -0.5)
    qpos = qi * tq + lax.broadcasted_iota(I32, (1, tq), 1)
    krow = lax.broadcasted_iota(I32, (LANE, 1), 0)

    def sbody(j, carry):
        for i in range(r):
            c = j * r + i
            k0 = pl.multiple_of(c * LANE, LANE)
            k2 = ik2_ref[pl.ds(k0, LANE), :].astype(BF16)
            rt = jnp.maximum(_dot_nt(k2, xq), 0.0)
            s = rt[:, :tq] * iwt[0:1]
            for h in range(1, H_IDX):
                s = s + rt[:, h * tq:(h + 1) * tq] * iwt[h:h + 1]
            sct_ref[c] = jnp.where(k0 + krow <= qpos, s, -jnp.inf)
        return carry

    lax.fori_loop(0, qi // r + 1, sbody, 0)

    def select(n):
        def run():
            idx = (lax.broadcasted_iota(I32, (n, LANE, 1), 0) * LANE
                   + lax.broadcasted_iota(I32, (n, LANE, 1), 1))
            sbt_ref[:n] = _select_topk(sct_ref[:n], idx, topk, (0, 1), (n * LANE - 1).bit_length(), 1)
            if n < nc:
                sbt_ref[n:] = jnp.full((nc - n, LANE, tq), NEG, F32)
        return run

    lax.switch(qi // r, [select((g + 1) * r) for g in range(nc // r)])

    aq = aq_ref[...] * (DA ** -0.5)
    q = jnp.concatenate([aq[:, h * DA:(h + 1) * DA] for h in range(HA)], axis=0).astype(BF16)
    _flash_init(m_ref, l_ref, acc_ref)

    def abody(j, carry):
        k0 = pl.multiple_of(j * tk, tk)
        k = ak_ref[pl.ds(k0, tk), :].astype(BF16)
        st = _dot_nt(k, q)
        z = _zone(j, qi, r)
        sbc = jnp.concatenate([sbt_ref[r * j + i] for i in range(r)], axis=0)
        st = jnp.concatenate([st[:, h * tq:(h + 1) * tq] + bias_ref[z, h] + sbc for h in range(HA)], axis=1)
        _flash_update_t(st, lambda p: _dot(avt_ref[j], p), m_ref, l_ref, acc_ref)
        return carry

    lax.fori_loop(0, qi // r + 1, abody, 0)
    ot = acc_ref[...] / l_ref[...]
    for h in range(HA):
        o_ref[:, h * DA:(h + 1) * DA] = jnp.transpose(ot[:, h * tq:(h + 1) * tq])


def dsa_prompt(h, avt, bias, nb, seq, tk, topk):
    tq = LANE
    nq = seq // tq
    rows = HA * tq
    assert avt.shape[2] == tk
    return pl.pallas_call(
        functools.partial(_dsa_p_kernel, tq=tq, tk=tk, nc=nq, topk=topk),
        grid=(nb, nq),
        in_specs=[pl.BlockSpec((tq, HA * DA), lambda b, i: (b * nq + i, C_AQ // (HA * DA))),
                  pl.BlockSpec((tq, H_IDX * D_IDX), lambda b, i: (b * nq + i, C_IQ // (H_IDX * D_IDX))),
                  pl.BlockSpec((tq, LANE), lambda b, i: (b * nq + i, C_IW // LANE)),
                  pl.BlockSpec((seq, LANE), lambda b, i: (b, C_IK2 // LANE)),
                  pl.BlockSpec((seq, DA), lambda b, i: (b, C_AK // DA)),
                  pl.BlockSpec((seq // tk, DA, tk), lambda b, i: (b, 0, 0)),
                  pl.BlockSpec(bias.shape, lambda b, i: (0, 0, 0, 0))],
        out_specs=pl.BlockSpec((tq, HA * DA), lambda b, i: (b * nq + i, 0)),
        out_shape=jax.ShapeDtypeStruct((nb * seq, HA * DA), F32),
        scratch_shapes=[pltpu.VMEM((nq, LANE, tq), F32), pltpu.VMEM((nq, LANE, tq), F32),
                        pltpu.VMEM((1, rows), F32), pltpu.VMEM((1, rows), F32),
                        pltpu.VMEM((DA, rows), F32)],
        compiler_params=_cparams(("parallel", "arbitrary")),
        name="dsa_prompt",
    )(h, h, h, h, h, avt, bias)


def _page_dst(buf, lead, p, page_rows):
    if page_rows == 0:
        return buf.at[lead + (slice(None), pl.ds(p * PAGE, PAGE))]
    return buf.at[lead + (pl.ds(p * page_rows, page_rows),)]


def _paged_pipeline(pt_ref, n_pages, layer, pools, bufs, page_rows, sem, rows_per_step=1):
    b = pl.program_id(0)
    slot = b % 2

    def copies(step, sl):
        cps = []
        for rr in range(rows_per_step):
            lead = (sl,) if rows_per_step == 1 else (sl, rr)
            for p in range(n_pages):
                page = pt_ref[step * rows_per_step + rr, p]
                for a, (pool, buf, pr) in enumerate(zip(pools, bufs, page_rows)):
                    cps.append(pltpu.make_async_copy(pool.at[layer, page], _page_dst(buf, lead, p, pr),
                                                     sem.at[a, sl]))
        return cps

    @pl.when(b == 0)
    def _():
        for cp in copies(b, slot):
            cp.start()

    @pl.when(b + 1 < pl.num_programs(0))
    def _():
        for cp in copies(b + 1, 1 - slot):
            cp.start()

    for cp in copies(b, slot):
        cp.wait()
    return slot


def _pad_rows(x, rows):
    return jnp.concatenate([x, jnp.zeros((rows - x.shape[0], x.shape[1]), x.dtype)], axis=0)


def _new_key_mask(rows, nq):
    q = lax.broadcasted_iota(I32, (rows, 1), 0) % nq
    j = lax.broadcasted_iota(I32, (1, LANE), 1)
    return j <= q


def _key_chunks(past):
    cw = min(CW, past)
    assert past % cw == 0
    return [slice(c * cw, (c + 1) * cw) for c in range(past // cw)]


def _softmax_pv(s_ref, past, values, pv):
    s_all = s_ref[...]
    m = jnp.max(s_all, axis=-1, keepdims=True)
    acc = None
    l = None
    for kc in _key_chunks(past) + [None]:
        ks = slice(past, past + LANE) if kc is None else kc
        p = jnp.exp(s_ref[:, ks] - m)
        part = pv(p.astype(BF16), values(kc))
        lsum = jnp.sum(p, axis=-1, keepdims=True)
        acc = part if acc is None else acc + part
        l = lsum if l is None else l + lsum
    return acc / l


def _mla_s_kernel(pt_ref, qlat_ref, qpe_ref, nlat_ref, nkpe_ref, wuv_ref, lat_hbm, kpet_hbm, o_ref,
                  latbuf, kpetbuf, sem, s_ref, *, layer, n_pages, nq, scale):
    slot = _paged_pipeline(pt_ref, n_pages, layer, (lat_hbm, kpet_hbm), (latbuf, kpetbuf), (PAGE, 0), sem)
    past = n_pages * PAGE
    rows = HB * nq
    ql = (qlat_ref[...] * scale).reshape(rows, KV_LORA).astype(BF16)
    qp = (qpe_ref[...] * scale).reshape(rows, DR).astype(BF16)
    for ks in _key_chunks(past):
        s_ref[:, ks] = (_dot_nt(ql, latbuf[slot, ks, :].astype(BF16))
                        + _dot(qp, kpetbuf[slot, :, ks].astype(BF16)))
    nl = _pad_rows(nlat_ref[...], LANE).astype(BF16)
    npe = _pad_rows(nkpe_ref[...], LANE).astype(BF16)
    s_ref[:, past:] = jnp.where(_new_key_mask(rows, nq), _dot_nt(ql, nl) + _dot_nt(qp, npe), NEG)

    def values(ks):
        return nl if ks is None else latbuf[slot, ks, :].astype(BF16)

    o = _softmax_pv(s_ref, past, values, _dot)
    for h in range(HB):
        o_ref[:, h * DV:(h + 1) * DV] = _dot(o[h * nq:(h + 1) * nq].astype(BF16), wuv_ref[h])


def mla_sample(page_table, qlat, qpe, blat, bkpe, wuv, cache_lat, cache_kpet, layer, nq):
    nb, n_pages = page_table.shape
    past = n_pages * PAGE
    rows = HB * nq
    grid_spec = pltpu.PrefetchScalarGridSpec(
        num_scalar_prefetch=1,
        grid=(nb,),
        in_specs=[pl.BlockSpec((HB, nq, KV_LORA), lambda b, pt: (0, b, 0)),
                  pl.BlockSpec((HB, nq, DR), lambda b, pt: (0, b, 0)),
                  pl.BlockSpec((nq, KV_LORA), lambda b, pt: (b, 0)),
                  pl.BlockSpec((nq, DR), lambda b, pt: (b, 0)),
                  pl.BlockSpec((None,) + wuv.shape[1:], lambda b, pt: (layer, 0, 0, 0)),
                  pl.BlockSpec(memory_space=pl.ANY),
                  pl.BlockSpec(memory_space=pl.ANY)],
        out_specs=pl.BlockSpec((nq, HB * DV), lambda b, pt: (b, 0)),
        scratch_shapes=[pltpu.VMEM((2, past, KV_LORA), F32),
                        pltpu.VMEM((2, DR, past), F32),
                        pltpu.SemaphoreType.DMA((2, 2)),
                        pltpu.VMEM((rows, past + LANE), F32)])
    return pl.pallas_call(
        functools.partial(_mla_s_kernel, layer=layer, n_pages=n_pages, nq=nq, scale=(DN + DR) ** -0.5),
        grid_spec=grid_spec,
        out_shape=jax.ShapeDtypeStruct((nb * nq, HB * DV), F32),
        compiler_params=_cparams(("arbitrary",)),
        name="mla_sample",
    )(page_table, qlat, qpe, blat, bkpe, wuv, cache_lat, cache_kpet)


def _diff_s_kernel(pt_ref, cq_ref, nck_ref, ncv_ref, bias_ref, bnew_ref, lq1_ref, lk1_ref, lq2_ref, lk2_ref,
                   gsub_ref, ck_hbm, cv_hbm, o_ref, ckbuf, cvbuf, sem, s_ref, *, layer, n_pages, nq, lam_init):
    slot = _paged_pipeline(pt_ref, n_pages, layer, (ck_hbm, cv_hbm), (ckbuf, cvbuf),
                           (HC_KV * PAGE, HC_KV * PAGE), sem)
    past = n_pages * PAGE
    rows = 2 * HC * nq
    q = _diff_queries(cq_ref[...], nq)

    def packed(buf, ks):
        n = ks.stop - ks.start
        heads = [buf[slot, pl.ds(HC_KV * ks.start + kvh, n, stride=HC_KV), :] for kvh in range(HC_KV)]
        return jnp.concatenate(heads, axis=1).astype(BF16)

    for ks in _key_chunks(past):
        s_ref[:, ks] = _dot_nt(q, packed(ckbuf, ks)) + bias_ref[:, ks]
    nk = _pad_rows(nck_ref[...], LANE).astype(BF16)
    nv = _pad_rows(ncv_ref[...], LANE).astype(BF16)
    s_ref[:, past:] = jnp.where(_new_key_mask(rows, nq), _dot_nt(q, nk) + bnew_ref[...], NEG)

    def values(ks):
        return nv if ks is None else packed(cvbuf, ks)

    o = _softmax_pv(s_ref, past, values, lambda p, v: _diff_pv(p, v, HC * nq))
    lam = _diff_lambda(lq1_ref, lk1_ref, lq2_ref, lk2_ref, lam_init)
    _diff_finish(o, nq, lam, gsub_ref[...], lam_init, o_ref)


def diff_sample(page_table, h, bias, bias_new, lams, gsub, cache_ck, cache_cv, layer, nq, lam_init):
    nb, n_pages = page_table.shape
    past = n_pages * PAGE
    rows = 2 * HC * nq
    vec = pl.BlockSpec((1, DC_QK), lambda b, pt: (0, 0))
    grid_spec = pltpu.PrefetchScalarGridSpec(
        num_scalar_prefetch=1,
        grid=(nb,),
        in_specs=[pl.BlockSpec((nq, HC * LANE), lambda b, pt: (b, C_CQ // (HC * LANE))),
                  pl.BlockSpec((nq, HC_KV * LANE), lambda b, pt: (b, C_CK // (HC_KV * LANE))),
                  pl.BlockSpec((nq, HC_KV * DC_V), lambda b, pt: (b, C_CV // (HC_KV * DC_V))),
                  pl.BlockSpec(bias.shape, lambda b, pt: (0, 0)),
                  pl.BlockSpec(bias_new.shape, lambda b, pt: (0, 0)),
                  vec, vec, vec, vec,
                  pl.BlockSpec((1, DC_V), lambda b, pt: (0, 0)),
                  pl.BlockSpec(memory_space=pl.ANY),
                  pl.BlockSpec(memory_space=pl.ANY)],
        out_specs=pl.BlockSpec((nq, HC * DC_V), lambda b, pt: (b, 0)),
        scratch_shapes=[pltpu.VMEM((2, past * HC_KV, LANE), F32),
                        pltpu.VMEM((2, past * HC_KV, DC_V), F32),
                        pltpu.SemaphoreType.DMA((2, 2)),
                        pltpu.VMEM((rows, past + LANE), F32)])
    return pl.pallas_call(
        functools.partial(_diff_s_kernel, layer=layer, n_pages=n_pages, nq=nq, lam_init=lam_init),
        grid_spec=grid_spec,
        out_shape=jax.ShapeDtypeStruct((nb * nq, HC * DC_V), F32),
        compiler_params=_cparams(("arbitrary",)),
        name="diff_sample",
    )(page_table, h, h, h, bias, bias_new, *lams, gsub.reshape(1, -1), cache_ck, cache_cv)


def _dsa_sel_kernel(pt_ref, iq_ref, iw_ref, nik_ref, kidxt_hbm, o_ref, kidxtbuf, sem, sc_ref,
                    *, layer, n_pages, nq, topk, bb):
    slot = _paged_pipeline(pt_ref, n_pages, layer, (kidxt_hbm,), (kidxtbuf,), (0,), sem, bb)
    past = n_pages * PAGE
    for rr in range(bb):
        rs = slice(rr * nq, (rr + 1) * nq)
        iq = iq_ref[rs, :]
        iw = iw_ref[rs, :] * (D_IDX ** -0.5 * H_IDX ** -0.5)
        xq = jnp.concatenate([iq[:, h * D_IDX:(h + 1) * D_IDX] for h in range(H_IDX)], axis=0).astype(BF16)
        iwc = jnp.concatenate([iw[:, h:h + 1] for h in range(H_IDX)], axis=0)

        def scores(r, iwc=iwc):
            return jnp.sum((jnp.maximum(r, 0.0) * iwc).reshape(H_IDX, nq, r.shape[1]), axis=0)

        lead = (slot,) if bb == 1 else (slot, rr)
        for ks in _key_chunks(past):
            sc_ref[rs, ks] = scores(_dot(xq, kidxtbuf[lead + (slice(None), ks)].astype(BF16)))
        s_new = scores(_dot_nt(xq, _pad_rows(nik_ref[rs, :][:, :D_IDX], LANE).astype(BF16)))
        sc_ref[rs, past:] = jnp.where(_new_key_mask(nq, nq), s_new, -jnp.inf)

    idx = lax.broadcasted_iota(I32, (1, past + LANE), 1)
    o_ref[...] = _select_topk(sc_ref[...], idx, topk, (1,), (past + LANE - 1).bit_length(), 1)


def dsa_select_sample(page_table, h, cache_kidxt, layer, nq, topk):
    nb, n_pages = page_table.shape
    past = n_pages * PAGE
    bb = min(nb, 8)
    assert nb % bb == 0
    rows = bb * nq
    grid_spec = pltpu.PrefetchScalarGridSpec(
        num_scalar_prefetch=1,
        grid=(nb // bb,),
        in_specs=[pl.BlockSpec((rows, H_IDX * D_IDX), lambda g, pt: (g, C_IQ // (H_IDX * D_IDX))),
                  pl.BlockSpec((rows, LANE), lambda g, pt: (g, C_IW // LANE)),
                  pl.BlockSpec((rows, LANE), lambda g, pt: (g, C_IK2 // LANE)),
                  pl.BlockSpec(memory_space=pl.ANY)],
        out_specs=pl.BlockSpec((rows, past + LANE), lambda g, pt: (g, 0)),
        scratch_shapes=[pltpu.VMEM((2, D_IDX, past) if bb == 1 else (2, bb, D_IDX, past), F32),
                        pltpu.SemaphoreType.DMA((1, 2)),
                        pltpu.VMEM((rows, past + LANE), F32)])
    return pl.pallas_call(
        functools.partial(_dsa_sel_kernel, layer=layer, n_pages=n_pages, nq=nq, topk=topk, bb=bb),
        grid_spec=grid_spec,
        out_shape=jax.ShapeDtypeStruct((nb * nq, past + LANE), F32),
        compiler_params=_cparams(("arbitrary",)),
        name="dsa_select",
    )(page_table, h, h, h, cache_kidxt)


def _dsa_s_kernel(pt_ref, aq_ref, sb_ref, nak_ref, nav_ref, bias_ref, bnew_ref,
                  ak_hbm, av_hbm, o_ref, akbuf, avbuf, sem, s_ref, *, layer, n_pages, nq):
    slot = _paged_pipeline(pt_ref, n_pages, layer, (ak_hbm, av_hbm), (akbuf, avbuf), (PAGE, PAGE), sem)
    past = n_pages * PAGE
    sb = sb_ref[...]
    aq = aq_ref[...] * (DA ** -0.5)
    q = jnp.concatenate([aq[:, h * DA:(h + 1) * DA] for h in range(HA)], axis=0).astype(BF16)
    rows = HA * nq

    def logits(k, bias, sbc):
        return ((_dot_nt(q, k)).reshape(HA, nq, k.shape[0]) + bias + sbc[None]).reshape(rows, k.shape[0])

    for ks in _key_chunks(past):
        s_ref[:, ks] = logits(akbuf[slot, ks, :].astype(BF16), bias_ref[:, :, ks], sb[:, ks])
    s_ref[:, past:] = logits(_pad_rows(nak_ref[...], LANE).astype(BF16), bnew_ref[...], sb[:, past:])
    nv = _pad_rows(nav_ref[...], LANE).astype(BF16)

    def values(ks):
        return nv if ks is None else avbuf[slot, ks, :].astype(BF16)

    o = _softmax_pv(s_ref, past, values, _dot)
    for h in range(HA):
        o_ref[:, h * DA:(h + 1) * DA] = o[h * nq:(h + 1) * nq]


def dsa_sample(page_table, h, sel, bias, bias_new, cache_ak, cache_av, layer, nq):
    nb, n_pages = page_table.shape
    past = n_pages * PAGE
    rows = HA * nq
    grid_spec = pltpu.PrefetchScalarGridSpec(
        num_scalar_prefetch=1,
        grid=(nb,),
        in_specs=[pl.BlockSpec((nq, HA * DA), lambda b, pt: (b, C_AQ // (HA * DA))),
                  pl.BlockSpec((nq, past + LANE), lambda b, pt: (b, 0)),
                  pl.BlockSpec((nq, DA), lambda b, pt: (b, C_AK // DA)),
                  pl.BlockSpec((nq, DA), lambda b, pt: (b, C_AV // DA)),
                  pl.BlockSpec(bias.shape, lambda b, pt: (0, 0, 0)),
                  pl.BlockSpec(bias_new.shape, lambda b, pt: (0, 0, 0)),
                  pl.BlockSpec(memory_space=pl.ANY),
                  pl.BlockSpec(memory_space=pl.ANY)],
        out_specs=pl.BlockSpec((nq, HA * DA), lambda b, pt: (b, 0)),
        scratch_shapes=[pltpu.VMEM((2, past, DA), F32),
                        pltpu.VMEM((2, past, DA), F32),
                        pltpu.SemaphoreType.DMA((2, 2)),
                        pltpu.VMEM((rows, past + LANE), F32)])
    return pl.pallas_call(
        functools.partial(_dsa_s_kernel, layer=layer, n_pages=n_pages, nq=nq),
        grid_spec=grid_spec,
        out_shape=jax.ShapeDtypeStruct((nb * nq, HA * DA), F32),
        compiler_params=_cparams(("arbitrary",)),
        name="dsa_sample",
    )(page_table, h, sel, h, h, bias, bias_new, cache_ak, cache_av)


def _out_kernel(a_ref, b_ref, c_ref, x_ref, w_ref, g_ref, o_ref):
    na = HA * HEAD_DIM
    nab = na + HB * DV
    y = (_dot(a_ref[...].astype(BF16), w_ref[:na, :])
         + _dot(b_ref[...].astype(BF16), w_ref[na:nab, :])
         + _dot(c_ref[...].astype(BF16), w_ref[nab:, :]))
    o_ref[...] = x_ref[...] + _rms(y, g_ref[...])


def out_project(a, b, c, x, w, layer, g):
    rows = x.shape[0]
    tm = _row_tile(rows, 256)
    row = lambda i: (i, 0)
    return pl.pallas_call(
        _out_kernel,
        grid=(rows // tm,),
        in_specs=[pl.BlockSpec((tm, a.shape[1]), row), pl.BlockSpec((tm, b.shape[1]), row),
                  pl.BlockSpec((tm, c.shape[1]), row), pl.BlockSpec((tm, D_MODEL), row),
                  pl.BlockSpec((None,) + w.shape[1:], lambda i: (layer, 0, 0)),
                  pl.BlockSpec((1, D_MODEL), lambda i: (0, 0))],
        out_specs=pl.BlockSpec((tm, D_MODEL), row),
        out_shape=jax.ShapeDtypeStruct((rows, D_MODEL), F32),
        compiler_params=_cparams(("parallel",)),
        name="out_proj",
    )(a, b, c, x, w, g.reshape(1, -1))


def _ffn_kernel(x_ref, gpre_ref, wg_ref, wu_ref, wd_ref, gpost_ref, o_ref, xn_ref, acc_ref):
    j = pl.program_id(1)

    @pl.when(j == 0)
    def _():
        xn_ref[...] = _rms(x_ref[...], gpre_ref[...]).astype(BF16)
        acc_ref[...] = jnp.zeros(acc_ref.shape, F32)

    xn = xn_ref[...]
    a = _dot(xn, wg_ref[...])
    u = _dot(xn, wu_ref[...])
    hmid = (a * (1.0 / (1.0 + jnp.exp(-a)))) * u
    acc_ref[...] += _dot(hmid.astype(BF16), wd_ref[...])

    @pl.when(j == pl.num_programs(1) - 1)
    def _():
        o_ref[...] = x_ref[...] + _rms(acc_ref[...], gpost_ref[...])


def ffn(x, g_pre, wg, wu, wd, layer, g_post, tf):
    rows = x.shape[0]
    dff = wg.shape[2]
    tm = _row_tile(rows, 512)
    return pl.pallas_call(
        _ffn_kernel,
        grid=(rows // tm, dff // tf),
        in_specs=[pl.BlockSpec((tm, D_MODEL), lambda i, j: (i, 0)),
                  pl.BlockSpec((1, D_MODEL), lambda i, j: (0, 0)),
                  pl.BlockSpec((None, D_MODEL, tf), lambda i, j: (layer, 0, j)),
                  pl.BlockSpec((None, D_MODEL, tf), lambda i, j: (layer, 0, j)),
                  pl.BlockSpec((None, tf, D_MODEL), lambda i, j: (layer, j, 0)),
                  pl.BlockSpec((1, D_MODEL), lambda i, j: (0, 0))],
        out_specs=pl.BlockSpec((tm, D_MODEL), lambda i, j: (i, 0)),
        out_shape=jax.ShapeDtypeStruct((rows, D_MODEL), F32),
        scratch_shapes=[pltpu.VMEM((tm, D_MODEL), BF16), pltpu.VMEM((tm, D_MODEL), F32)],
        compiler_params=_cparams(("parallel", "arbitrary")),
        name="ffn",
    )(x, g_pre.reshape(1, -1), wg, wu, wd, g_post.reshape(1, -1))


def _rope_tables(pos, rows):
    half = DR // 2
    inv = jnp.power(ROPE_THETA, -jnp.arange(half, dtype=F32) / half)
    ang = pos.astype(F32)[:, None] * inv[None, :]
    cos, sin = jnp.cos(ang), jnp.sin(ang)
    c = jnp.tile(jnp.concatenate([cos, cos], axis=-1), (rows // pos.shape[0], HB))
    s = jnp.tile(jnp.concatenate([-sin, sin], axis=-1), (rows // pos.shape[0], HB))
    return c, s


def _prompt_buckets(tq, tk):
    i = np.arange(tq)[None, :]
    j = np.arange(tk)[:, None]
    zones = [i - j + z * tq for z in range(tk // tq + 1)] + [np.full((tk, tq), MAX_DISTANCE, np.int64)]
    return _bucket_np(np.concatenate(zones, axis=0))


def _sample_buckets(nq, past):
    qs = np.arange(nq)[:, None]
    j = np.arange(PAGE)[None, :]
    last = (past + qs) - (past - PAGE + j)
    new = np.maximum(qs - j, 0)
    return _bucket_np(np.concatenate([last, new], axis=0))


def kernel(x_prompt, x_sample, cache_a_k, cache_a_v, cache_a_kidx, cache_b_latent, cache_b_krope, cache_c_k, cache_c_v, page_table, rel_bias, norm_pre_mix, w_in, q_a_norm, w_q_b, kv_a_norm, w_uk, w_uv, lambda_q1, lambda_k1, lambda_q2, lambda_k2, c_subln, w_out, norm_post_mix, norm_pre_ffn, w_gate, w_up, w_down, norm_post_ffn):
    nb_p, seq, _ = x_prompt.shape
    nb_s, nq_s, _ = x_sample.shape
    depth = w_in.shape[0]
    n_pages = page_table.shape[1]
    past = n_pages * PAGE
    topk_p = min(TOPK_MAX, seq // 4)
    topk_s = min(TOPK_MAX, (past + nq_s) // 4)
    tq = TQ
    tk = min(TK, seq)
    assert seq % tk == 0 and tk % tq == 0 and nq_s == 8

    perm = _in_perm()
    w_in_r = _reorder_cols(w_in, perm).astype(BF16)
    w_qb_r = _reorder_cols(w_q_b, _qb_perm()).astype(BF16)
    w_uk_r = jnp.transpose(w_uk, (0, 2, 3, 1)).astype(BF16)
    w_uv_r = jnp.transpose(w_uv, (0, 2, 1, 3)).astype(BF16)
    w_uv_t = jnp.transpose(w_uv, (0, 2, 3, 1)).astype(BF16)
    w_out_b = w_out.astype(BF16)
    w_gate_b = w_gate.astype(BF16)
    w_up_b = w_up.astype(BF16)
    w_down_b = w_down.astype(BF16)
    cache_kidxt = jnp.swapaxes(cache_a_kidx, 2, 3)
    cache_kpet = jnp.swapaxes(cache_b_krope, 2, 3)
    cache_ck2 = cache_c_k.reshape(cache_c_k.shape[:2] + (PAGE * HC_KV, 2 * DC_QK))
    cache_cv2 = cache_c_v.reshape(cache_c_v.shape[:2] + (PAGE * HC_KV, DC_V))

    rows_p = nb_p * seq
    rows_s = nb_s * nq_s
    tm_p = _row_tile(rows_p, 512)
    tm_s = _row_tile(rows_s, 512)
    cos_p, sin_p = _rope_tables(jnp.arange(seq, dtype=I32), max(seq, tm_p))
    cos_s, sin_s = _rope_tables(past + jnp.arange(nq_s, dtype=I32), max(nq_s, tm_s))

    nz = tk // tq + 2
    zones = t5_expand(rel_bias, jnp.asarray(_prompt_buckets(tq, tk)), tk).reshape(HA + HC, nz, tk, tq)
    zones = jnp.transpose(zones, (1, 0, 2, 3))
    bias_a_p, bias_c_p = zones[:, :HA], zones[:, HA:]
    tiles = t5_expand(rel_bias, jnp.asarray(_sample_buckets(nq_s, past)), 2 * nq_s)
    last, new = tiles[:, :nq_s], tiles[:, nq_s:]
    far = jnp.broadcast_to(zones[nz - 1, :, :1, :1], (HA + HC, nq_s, past - PAGE))
    full = jnp.concatenate([far, last], axis=-1)
    bias_a_s, bias_a_new = full[:HA], new[:HA]
    rep = lambda t: jnp.repeat(t[HA:], 2, axis=0).reshape(2 * HC * nq_s, t.shape[-1])
    bias_c_s, bias_c_new = rep(full), rep(new)

    hp = x_prompt.reshape(rows_p, D_MODEL)
    hs = x_sample.reshape(rows_s, D_MODEL)
    new_p = [[] for _ in range(7)]
    new_s = [[] for _ in range(7)]
    for l in range(depth):
        lam_init = 0.8 - 0.6 * math.exp(-0.3 * l)
        lams = tuple(v[l].reshape(1, DC_QK) for v in (lambda_q1, lambda_k1, lambda_q2, lambda_k2))
        outs = []
        for grp, (x, cos_t, sin_t) in enumerate(((hp, cos_p, sin_p), (hs, cos_s, sin_s))):
            h = norm_matmul(x, norm_pre_mix[l], w_in_r, l, IN_COLS_R // 3)
            qlat, qpe, blat, bkpe, *vts = post_project(h, cos_t, sin_t, q_a_norm[l], kv_a_norm[l], w_qb_r,
                                                       w_uk_r, l, grp == 0)
            if grp == 0:
                latt, avt, cvt = vts
                a_o = dsa_prompt(h, avt, bias_a_p, nb_p, seq, tk, topk_p)
                b_o = mla_prompt(qlat, qpe, blat, bkpe, latt, w_uv_t, l, nb_p, seq, tq, tk)
                c_o = diff_prompt(h, cvt, bias_c_p, lams, c_subln[l], nb_p, seq, tq, tk, lam_init)
            else:
                sel = dsa_select_sample(page_table, h, cache_kidxt, l, nq_s, topk_s)
                a_o = dsa_sample(page_table, h, sel, bias_a_s, bias_a_new, cache_a_k, cache_a_v, l, nq_s)
                b_o = mla_sample(page_table, qlat, qpe, blat, bkpe, w_uv_r, cache_b_latent, cache_kpet,
                                 l, nq_s)
                c_o = diff_sample(page_table, h, bias_c_s, bias_c_new, lams, c_subln[l], cache_ck2, cache_cv2,
                                  l, nq_s, lam_init)
            x1 = out_project(a_o, b_o, c_o, x, w_out_b, l, norm_post_mix[l])
            outs.append(ffn(x1, norm_pre_ffn[l], w_gate_b, w_up_b, w_down_b, l, norm_post_ffn[l], 512))
            rows = (h[:, C_AK:C_AK + DA], h[:, C_AV:C_AV + DA], h[:, C_IK2:C_IK2 + D_IDX], blat, bkpe,
                    h[:, C_CK:C_CK + HC_KV * LANE], h[:, C_CV:C_CV + HC_KV * DC_V])
            for i in range(7):
                (new_p if grp == 0 else new_s)[i].append(rows[i])
        hp, hs = outs

    def stack(parts, lead, tail):
        return jnp.stack(parts).reshape((depth,) + lead + tail)

    tails = ((DA,), (DA,), (D_IDX,), (KV_LORA,), (DR,), (HC_KV, 2 * DC_QK), (HC_KV, DC_V))
    outs_p = tuple(stack(new_p[i], (nb_p, seq), tails[i]) for i in range(7))
    outs_s = tuple(stack(new_s[i], (nb_s, nq_s), tails[i]) for i in range(7))
    return (hp.reshape(x_prompt.shape), hs.reshape(x_sample.shape)) + outs_p + outs_s
```
